```python
import math
import jax
import jax.numpy as jnp
from jax import lax
import numpy as np

D_MODEL = 2048
BATCH = 4
SEQ = 2048
DEPTH = 2
DEC_BATCH = 128
DEC_SEQ = 1
PAST_LEN = 2048
PAGE_SIZE = 128

N_EVEN = (DEPTH + 1) // 2
N_ODD = DEPTH // 2
BLOCK = 128
SB_HEADS = 8
SB_HEAD_DIM = D_MODEL // 16
SB_W = SB_HEADS * SB_HEAD_DIM
SB_BIAS_INIT = -6.0
RET_HEADS = 8
RET_DK = D_MODEL // 16
RET_DV = 2 * RET_DK
RET_QK_W = RET_HEADS * RET_DK
RET_V_W = RET_HEADS * RET_DV
RET_ROPE_BASE = 10000.0
AB_IN = 3 * SB_W + 2 * RET_QK_W + 2 * RET_V_W
AB_OUT = SB_W + RET_V_W
SSM_EXPAND = 2
SSM_D_INNER = SSM_EXPAND * D_MODEL
SSM_HEAD_DIM = 64
SSM_HEADS = SSM_D_INNER // SSM_HEAD_DIM
SSM_GROUPS = 8
SSM_HPG = SSM_HEADS // SSM_GROUPS
SSM_D_STATE = 128
SSM_CONV = 4
SSM_CONV_DIM = SSM_D_INNER + 2 * SSM_GROUPS * SSM_D_STATE
SSM_IN = SSM_D_INNER + SSM_CONV_DIM + SSM_HEADS
D_FF = 11 * D_MODEL // 4
FFN_CONV = 3
LN_EPS = 1e-5
NORM_EPS = 1e-5
DN_ALPHA = (2 * DEPTH) ** 0.25
DN_BETA = (8 * DEPTH) ** -0.25

kernel_name = 'hybrid_stickbreak_retnet_mamba2_convffn_step'


def _chunk_len(length):
    return length if length <= BLOCK else math.gcd(length, BLOCK)


def _layer_norm(x, g, b):
    xf = x.astype(jnp.float32)
    mu = jnp.mean(xf, axis=-1, keepdims=True)
    var = jnp.mean(jnp.square(xf - mu), axis=-1, keepdims=True)
    return ((xf - mu) * lax.rsqrt(var + LN_EPS) * g.astype(jnp.float32) + b.astype(jnp.float32)).astype(x.dtype)


def _rms(x):
    xf = x.astype(jnp.float32)
    return xf * lax.rsqrt(jnp.mean(jnp.square(xf), axis=-1, keepdims=True) + NORM_EPS)


def _causal_dwconv(x, prev, w, b):
    width, length = w.shape[0], x.shape[1]
    xp = jnp.concatenate([prev.astype(x.dtype), x], axis=1)
    out = b + xp[:, 0:length] * w[0]
    for tap in range(1, width):
        out = out + xp[:, tap:tap + length] * w[tap]
    return out, xp[:, length:]


def _rotary(x, pos):
    half = x.shape[-1] // 2
    inv_freq = RET_ROPE_BASE ** (-jnp.linspace(0.0, 1.0, half, dtype=jnp.float32))
    ang = pos.astype(jnp.float32)[:, None] * inv_freq[None, :]
    cos = jnp.cos(ang)[None, :, None, :]
    sin = jnp.sin(ang)[None, :, None, :]
    xf = x.astype(jnp.float32)
    x1, x2 = xf[..., :half], xf[..., half:]
    return jnp.concatenate([x1 * cos - x2 * sin, x1 * sin + x2 * cos], axis=-1)


def _stick_breaking(q, k, v, q_start, bias):
    lq = q.shape[1]
    scale = q.shape[-1] ** -0.5
    bias_f = bias.astype(jnp.float32)[None, :, None, None]
    outs = []
    for i0 in range(0, lq, BLOCK):
        qb = q[:, i0:i0 + BLOCK]
        nq = qb.shape[1]
        nk = q_start + i0 + nq
        kb, vb = k[:, :nk], v[:, :nk]
        z = jnp.einsum('bqhd,bkhd->bhqk', qb, kb).astype(jnp.float32) * scale + bias_f
        qpos = q_start + i0 + jnp.arange(nq)
        kpos = jnp.arange(nk)
        mask = (kpos[None, :] < qpos[:, None])[None, None]
        log_fail = jnp.where(mask, -jax.nn.softplus(z), 0.0)
        later = lax.cumsum(log_fail, axis=3, reverse=True) - log_fail
        w = jnp.where(mask, jnp.exp(jax.nn.log_sigmoid(z) + later), 0.0)
        outs.append(jnp.einsum('bhqk,bkhd->bqhd', w.astype(vb.dtype), vb))
    return jnp.concatenate(outs, axis=1)


def _retention(q, k, v, s0, log_gamma):
    bsz, length, heads, dk = q.shape
    dv = v.shape[-1]
    c = _chunk_len(length)
    n = length // c
    qc = q.reshape(bsz, n, c, heads, dk)
    kc = k.reshape(bsz, n, c, heads, dk)
    vc = v.astype(jnp.float32).reshape(bsz, n, c, heads, dv)
    idx = jnp.arange(c, dtype=jnp.float32)
    gap = idx[:, None] - idx[None, :]
    decay = jnp.where(gap >= 0, jnp.exp(log_gamma[:, None, None] * jnp.maximum(gap, 0.0)), 0.0)
    scores = jnp.einsum('bcihd,bcjhd->bchij', qc, kc) * decay
    out = jnp.einsum('bchij,bcjhe->bcihe', scores, vc)
    to_end = jnp.exp(log_gamma[:, None] * (c - 1.0 - idx)[None, :])
    contrib = jnp.einsum('bcjhd,hj,bcjhe->bchde', kc, to_end, vc)
    chunk_decay = jnp.exp(log_gamma * c)[:, None, None]

    def step(s, ct):
        return s * chunk_decay + ct, s

    s_last, s_start = lax.scan(step, s0.astype(jnp.float32), jnp.moveaxis(contrib, 1, 0))
    s_start = jnp.moveaxis(s_start, 0, 1)
    from_start = jnp.exp(log_gamma[:, None] * (idx + 1.0)[None, :])
    out = out + jnp.einsum('bcihd,hi,bchde->bcihe', qc, from_start, s_start)
    return out.reshape(bsz, length, heads, dv), s_last


def _ssd_scan(x, dt, a, bm, cm, d_skip, h0):
    bsz, length = x.shape[:2]
    c = _chunk_len(length)
    n = length // c
    xf = x.astype(jnp.float32).reshape(bsz, n, c, *x.shape[2:])
    dtc = dt.reshape(bsz, n, c, *dt.shape[2:])
    bc = bm.astype(jnp.float32).reshape(bsz, n, c, *bm.shape[2:])
    cc = cm.astype(jnp.float32).reshape(bsz, n, c, *cm.shape[2:])
    cum = jnp.cumsum(dtc * a, axis=2)
    causal = jnp.tril(jnp.ones((c, c), dtype=bool))[None, None, :, :, None, None]
    seg = cum[:, :, :, None] - cum[:, :, None, :]
    decay = jnp.exp(jnp.where(causal, seg, -jnp.inf))
    cb = jnp.einsum('bcigs,bcjgs->bcijg', cc, bc)
    w = cb[..., None] * decay * dtc[:, :, None]
    y = jnp.einsum('bcijge,bcjgep->bcigep', w, xf)
    to_end = jnp.exp(cum[:, :, -1:] - cum) * dtc
    contrib = jnp.einsum('bcjgs,bcjge,bcjgep->bcgeps', bc, to_end, xf)
    chunk_decay = jnp.exp(cum[:, :, -1])

    def step(h, inp):
        cd, ct = inp
        return h * cd[..., None, None] + ct, h

    h_last, h_start = lax.scan(step, h0.astype(jnp.float32),
                               (jnp.moveaxis(chunk_decay, 1, 0), jnp.moveaxis(contrib, 1, 0)))
    h_start = jnp.moveaxis(h_start, 0, 1)
    y = y + jnp.einsum('bcigs,bcige,bcgeps->bcigep', cc, jnp.exp(cum), h_start)
    y = y + d_skip.astype(jnp.float32)[:, :, None] * xf
    return y.reshape(bsz, length, *x.shape[2:]), h_last


def _ab_mixer(x, pos0, past_k, past_v, ret_s0, w_in, w_out, sb_bias):
    bsz, length, _ = x.shape
    proj = x @ w_in
    cuts = [SB_W, 2 * SB_W, 3 * SB_W, 3 * SB_W + RET_QK_W, 3 * SB_W + 2 * RET_QK_W,
            3 * SB_W + 2 * RET_QK_W + RET_V_W]
    qa, ka, va, qr, kr, vr, gr = jnp.split(proj, cuts, axis=-1)
    qa = qa.reshape(bsz, length, SB_HEADS, SB_HEAD_DIM)
    ka = ka.reshape(bsz, length, SB_HEADS, SB_HEAD_DIM)
    va = va.reshape(bsz, length, SB_HEADS, SB_HEAD_DIM)
    if past_k is None:
        k_all, v_all = ka, va
    else:
        k_all = jnp.concatenate([past_k.astype(ka.dtype), ka], axis=1)
        v_all = jnp.concatenate([past_v.astype(va.dtype), va], axis=1)
    ya = _stick_breaking(qa, k_all, v_all, pos0, sb_bias)
    pos = pos0 + jnp.arange(length)
    log_gamma = jnp.log(1.0 - 2.0 ** (-5.0 - jnp.arange(RET_HEADS, dtype=jnp.float32)))
    qr = _rotary(qr.reshape(bsz, length, RET_HEADS, RET_DK), pos)
    kr = _rotary(kr.reshape(bsz, length, RET_HEADS, RET_DK), pos) * (RET_DK ** -0.5)
    yr, s_new = _retention(qr, kr, vr.reshape(bsz, length, RET_HEADS, RET_DV), ret_s0, log_gamma)
    yr = _rms(yr).reshape(bsz, length, RET_V_W) * jax.nn.silu(gr.astype(jnp.float32))
    merged = jnp.concatenate([ya.reshape(bsz, length, SB_W).astype(x.dtype), yr.astype(x.dtype)], axis=-1)
    return merged @ w_out, ka, va, s_new


def _ssd_mixer(x, h0, conv0, w_in, conv_w, conv_b, dt_bias, a_log, d_skip, norm_w, w_out):
    bsz, length, _ = x.shape
    proj = x @ w_in
    z, xbc, dt_raw = jnp.split(proj, [SSM_D_INNER, SSM_D_INNER + SSM_CONV_DIM], axis=-1)
    xbc, conv_new = _causal_dwconv(xbc, conv0, conv_w, conv_b)
    xbc = jax.nn.silu(xbc)
    xs, bm, cm = jnp.split(xbc, [SSM_D_INNER, SSM_D_INNER + SSM_GROUPS * SSM_D_STATE], axis=-1)
    dt = jax.nn.softplus(dt_raw.astype(jnp.float32) + dt_bias.astype(jnp.float32))
    a = -jnp.exp(a_log.astype(jnp.float32))
    y, h = _ssd_scan(xs.reshape(bsz, length, SSM_GROUPS, SSM_HPG, SSM_HEAD_DIM),
                     dt.reshape(bsz, length, SSM_GROUPS, SSM_HPG),
                     a.reshape(SSM_GROUPS, SSM_HPG),
                     bm.reshape(bsz, length, SSM_GROUPS, SSM_D_STATE),
                     cm.reshape(bsz, length, SSM_GROUPS, SSM_D_STATE),
                     d_skip.reshape(SSM_GROUPS, SSM_HPG),
                     h0.reshape(bsz, SSM_GROUPS, SSM_HPG, SSM_HEAD_DIM, SSM_D_STATE))
    y = y.reshape(bsz, length, SSM_D_INNER) * jax.nn.silu(z.astype(jnp.float32))
    y = _rms(y.reshape(bsz, length, SSM_GROUPS, SSM_D_INNER // SSM_GROUPS)).reshape(bsz, length, SSM_D_INNER)
    y = y * norm_w.astype(jnp.float32)
    return y.astype(x.dtype) @ w_out, h.reshape(bsz, SSM_HEADS, SSM_HEAD_DIM, SSM_D_STATE), conv_new


def _conv_ffn(x, conv0, w_up, conv_w, conv_b, w_down):
    h = x @ w_up
    h, conv_new = _causal_dwconv(h, conv0, conv_w, conv_b)
    g, u = jnp.split(h, 2, axis=-1)
    return (jax.nn.silu(g) * u) @ w_down, conv_new


def _trunk(x, pos0, cache_k, cache_v, page_table, ret0, ssm0, sconv0, fconv0,
           ln1_g, ln1_b, ln2_g, ln2_b, w_in_ab, w_out_ab, sb_bias, w_in_ssd, conv_w_ssd, conv_b_ssd,
           dt_bias, a_log, d_skip, norm_w_ssd, w_out_ssd, w_up, conv_w_ffn, conv_b_ffn, w_down):
    bsz = x.shape[0]
    new_k, new_v, ret_s, ssm_s, ssm_c, ffn_c = [], [], [], [], [], []
    for layer in range(DEPTH):
        idx = layer // 2
        if layer % 2 == 0:
            if cache_k is None:
                pk, pv = None, None
            else:
                pk = jnp.take(cache_k[idx], page_table, axis=0).reshape(bsz, pos0, SB_HEADS, SB_HEAD_DIM)
                pv = jnp.take(cache_v[idx], page_table, axis=0).reshape(bsz, pos0, SB_HEADS, SB_HEAD_DIM)
            m, k_new, v_new, s_new = _ab_mixer(x, pos0, pk, pv, ret0[idx], w_in_ab[idx], w_out_ab[idx],
                                               sb_bias[idx])
            new_k.append(k_new)
            new_v.append(v_new)
            ret_s.append(s_new)
        else:
            m, h_new, c_new = _ssd_mixer(x, ssm0[idx], sconv0[idx], w_in_ssd[idx], conv_w_ssd[idx],
                                         conv_b_ssd[idx], dt_bias[idx], a_log[idx], d_skip[idx],
                                         norm_w_ssd[idx], w_out_ssd[idx])
            ssm_s.append(h_new)
            ssm_c.append(c_new)
        x = _layer_norm(DN_ALPHA * x + m, ln1_g[layer], ln1_b[layer])
        f, fc = _conv_ffn(x, fconv0[layer], w_up[layer], conv_w_ffn[layer], conv_b_ffn[layer], w_down[layer])
        ffn_c.append(fc)
        x = _layer_norm(DN_ALPHA * x + f, ln2_g[layer], ln2_b[layer])
    return (x, jnp.stack(new_k), jnp.stack(new_v), jnp.stack(ret_s), jnp.stack(ssm_s),
            jnp.stack(ssm_c), jnp.stack(ffn_c))


def setup_inputs(seed: int = 0) -> dict:
    key = jax.random.key(seed)
    ks = iter(jax.random.split(key, 48))

    def nrm(shape, scale):
        return jax.random.normal(next(ks), shape, jnp.float32) * scale

    n_pages = PAST_LEN // PAGE_SIZE
    n_used = DEC_BATCH * n_pages
    n_phys = n_used + n_used // 4
    perm = jax.random.permutation(next(ks), n_phys)
    page_table = perm[:n_used].reshape(DEC_BATCH, n_pages).astype(jnp.int32)

    dt_init = jnp.exp(jax.random.uniform(next(ks), (N_ODD, SSM_HEADS), jnp.float32)
                      * (math.log(0.1) - math.log(0.001)) + math.log(0.001))
    dt_bias = dt_init + jnp.log(-jnp.expm1(-dt_init))
    a_log = jnp.log(jax.random.uniform(next(ks), (N_ODD, SSM_HEADS), jnp.float32, 1.0, 16.0))

    return {
        'x_prompt': nrm((BATCH, SEQ, D_MODEL), 1.0),
        'x_sample': nrm((DEC_BATCH, DEC_SEQ, D_MODEL), 1.0),
        'cache_k': nrm((N_EVEN, n_phys, PAGE_SIZE, SB_HEADS, SB_HEAD_DIM), 1.0),
        'cache_v': nrm((N_EVEN, n_phys, PAGE_SIZE, SB_HEADS, SB_HEAD_DIM), 1.0),
        'page_table': page_table,
        'state_ret': nrm((N_EVEN, DEC_BATCH, RET_HEADS, RET_DK, RET_DV), 0.5),
        'state_ssm': nrm((N_ODD, DEC_BATCH, SSM_HEADS, SSM_HEAD_DIM, SSM_D_STATE), 0.5),
        'state_ssm_conv': nrm((N_ODD, DEC_BATCH, SSM_CONV - 1, SSM_CONV_DIM), 1.0),
        'state_ffn_conv': nrm((DEPTH, DEC_BATCH, FFN_CONV - 1, 2 * D_FF), 1.0),
        'ln1_g': 1.0 + nrm((DEPTH, D_MODEL), 0.02),
        'ln1_b': nrm((DEPTH, D_MODEL), 0.02),
        'ln2_g': 1.0 + nrm((DEPTH, D_MODEL), 0.02),
        'ln2_b': nrm((DEPTH, D_MODEL), 0.02),
        'w_in_ab': nrm((N_EVEN, D_MODEL, AB_IN), D_MODEL ** -0.5),
        'w_out_ab': nrm((N_EVEN, AB_OUT, D_MODEL), AB_OUT ** -0.5 * DN_BETA),
        'sb_bias': SB_BIAS_INIT + nrm((N_EVEN, SB_HEADS), 0.1),
        'w_in_ssd': nrm((N_ODD, D_MODEL, SSM_IN), D_MODEL ** -0.5),
        'conv_w_ssd': nrm((N_ODD, SSM_CONV, SSM_CONV_DIM), SSM_CONV ** -0.5),
        'conv_b_ssd': nrm((N_ODD, SSM_CONV_DIM), 0.02),
        'dt_bias': dt_bias,
        'a_log': a_log,
        'd_skip': 1.0 + nrm((N_ODD, SSM_HEADS), 0.02),
        'norm_w_ssd': 1.0 + nrm((N_ODD, SSM_D_INNER), 0.02),
        'w_out_ssd': nrm((N_ODD, SSM_D_INNER, D_MODEL), SSM_D_INNER ** -0.5 * DN_BETA),
        'w_up': nrm((DEPTH, D_MODEL, 2 * D_FF), D_MODEL ** -0.5),
        'conv_w_ffn': nrm((DEPTH, FFN_CONV, 2 * D_FF), FFN_CONV ** -0.5),
        'conv_b_ffn': nrm((DEPTH, 2 * D_FF), 0.02),
        'w_down': nrm((DEPTH, D_FF, D_MODEL), D_FF ** -0.5 * DN_BETA),
    }


def reference(x_prompt, x_sample, cache_k, cache_v, page_table, state_ret, state_ssm, state_ssm_conv,
              state_ffn_conv, ln1_g, ln1_b, ln2_g, ln2_b, w_in_ab, w_out_ab, sb_bias, w_in_ssd, conv_w_ssd,
              conv_b_ssd, dt_bias, a_log, d_skip, norm_w_ssd, w_out_ssd, w_up, conv_w_ffn, conv_b_ffn, w_down):
    bp = x_prompt.shape[0]
    past_len = page_table.shape[1] * PAGE_SIZE
    ret0 = jnp.zeros((N_EVEN, bp, RET_HEADS, RET_DK, RET_DV), jnp.float32)
    ssm0 = jnp.zeros((N_ODD, bp, SSM_HEADS, SSM_HEAD_DIM, SSM_D_STATE), jnp.float32)
    sconv0 = jnp.zeros((N_ODD, bp, SSM_CONV - 1, SSM_CONV_DIM), x_prompt.dtype)
    fconv0 = jnp.zeros((DEPTH, bp, FFN_CONV - 1, 2 * D_FF), x_prompt.dtype)
    y_prompt, k_prompt, v_prompt, ret_prompt, ssm_prompt, ssm_conv_prompt, ffn_conv_prompt = _trunk(
        x_prompt, 0, None, None, None, ret0, ssm0, sconv0, fconv0,
        ln1_g, ln1_b, ln2_g, ln2_b, w_in_ab, w_out_ab, sb_bias, w_in_ssd, conv_w_ssd, conv_b_ssd,
        dt_bias, a_log, d_skip, norm_w_ssd, w_out_ssd, w_up, conv_w_ffn, conv_b_ffn, w_down)
    y_sample, k_sample, v_sample, ret_sample, ssm_sample, ssm_conv_sample, ffn_conv_sample = _trunk(
        x_sample, past_len, cache_k, cache_v, page_table, state_ret, state_ssm, state_ssm_conv, state_ffn_conv,
        ln1_g, ln1_b, ln2_g, ln2_b, w_in_ab, w_out_ab, sb_bias, w_in_ssd, conv_w_ssd, conv_b_ssd,
        dt_bias, a_log, d_skip, norm_w_ssd, w_out_ssd, w_up, conv_w_ffn, conv_b_ffn, w_down)
    return (y_prompt, y_sample, k_prompt, v_prompt, k_sample, v_sample, ret_prompt, ret_sample,
            ssm_prompt, ssm_sample, ssm_conv_prompt, ssm_conv_sample, ffn_conv_prompt, ffn_conv_sample)
```

```python
import functools
import math

import jax
import jax.numpy as jnp
from jax import lax
from jax.experimental import pallas as pl
from jax.experimental.pallas import tpu as pltpu

F32 = jnp.float32
BF16 = jnp.bfloat16

LANES = 128
SUBLANES = 8
VMEM_LIMIT_BYTES = 52 * 1024 * 1024

CHUNK = 128
LN_EPS = 1e-5
NORM_EPS = 1e-5
ROPE_BASE = 10000.0

NT_DIMS = (((1,), (1,)), ((), ()))
TN_DIMS = (((0,), (0,)), ((), ()))


def _params(*semantics):
    return pltpu.CompilerParams(dimension_semantics=semantics, vmem_limit_bytes=VMEM_LIMIT_BYTES)


def _softplus(z):
    return jnp.maximum(z, 0.0) + jnp.log1p(jnp.exp(-jnp.abs(z)))


def _silu(x):
    return x * jax.nn.sigmoid(x)


def _dot(a, b, dims=None):
    if dims is None:
        return jnp.dot(a, b, preferred_element_type=F32)
    return lax.dot_general(a, b, dims, preferred_element_type=F32)


def _dot_split(a, m, dims=None, terms=3):
    out = None
    rem = a
    for t in range(terms):
        part = rem.astype(BF16)
        if t + 1 < terms:
            rem = rem - part.astype(F32)
        out_t = _dot(part, m, dims)
        out = out_t if out is None else out + out_t
    return out


def _mm_kernel(x_ref, w_ref, o_ref):
    o_ref[...] = _dot(x_ref[...], w_ref[...]).astype(o_ref.dtype)


def _matmul(x, w, out_dtype, tm, tn):
    m, k = x.shape
    n = w.shape[1]
    tm, tn = min(tm, m), min(tn, n)
    assert m % tm == 0 and n % tn == 0
    return pl.pallas_call(
        _mm_kernel,
        grid=(n // tn, m // tm),
        in_specs=[pl.BlockSpec((tm, k), lambda j, i: (i, 0)),
                  pl.BlockSpec((k, tn), lambda j, i: (0, j))],
        out_specs=pl.BlockSpec((tm, tn), lambda j, i: (i, j)),
        out_shape=jax.ShapeDtypeStruct((m, n), out_dtype),
        compiler_params=_params("parallel", "parallel"),
        name="matmul",
    )(x, w)


def _mm_res_ln_kernel(a_ref, w_ref, r_ref, g_ref, b_ref, of_ref, ob_ref, acc_ref, *, nk, alpha):
    kk = pl.program_id(1)

    @pl.when(kk == 0)
    def _():
        acc_ref[...] = jnp.zeros_like(acc_ref)

    acc_ref[...] += _dot(a_ref[...], w_ref[...])

    @pl.when(kk == nk - 1)
    def _():
        y = alpha * r_ref[...] + acc_ref[...]
        mu = jnp.mean(y, axis=-1, keepdims=True)
        d = y - mu
        var = jnp.mean(d * d, axis=-1, keepdims=True)
        out = d * lax.rsqrt(var + LN_EPS) * g_ref[...] + b_ref[...]
        of_ref[...] = out
        ob_ref[...] = out.astype(BF16)


def _matmul_res_ln(a, w, resid, gain, bias, alpha, tm, tk):
    m, k = a.shape
    d = w.shape[1]
    tm = min(tm, m)
    assert m % tm == 0 and k % tk == 0
    nk = k // tk
    return pl.pallas_call(
        functools.partial(_mm_res_ln_kernel, nk=nk, alpha=alpha),
        grid=(m // tm, nk),
        in_specs=[pl.BlockSpec((tm, tk), lambda i, kk: (i, kk)),
                  pl.BlockSpec((tk, d), lambda i, kk: (kk, 0)),
                  pl.BlockSpec((tm, d), lambda i, kk: (i, 0)),
                  pl.BlockSpec((1, d), lambda i, kk: (0, 0)),
                  pl.BlockSpec((1, d), lambda i, kk: (0, 0))],
        out_specs=[pl.BlockSpec((tm, d), lambda i, kk: (i, 0)),
                   pl.BlockSpec((tm, d), lambda i, kk: (i, 0))],
        out_shape=[jax.ShapeDtypeStruct((m, d), F32), jax.ShapeDtypeStruct((m, d), BF16)],
        scratch_shapes=[pltpu.VMEM((tm, d), F32)],
        compiler_params=_params("parallel", "arbitrary"),
        name="matmul_res_ln",
    )(a, w, resid, gain.reshape(1, d), bias.reshape(1, d))


def _sb_prompt_kernel(bias_ref, q_ref, k_ref, v_ref, tri_ref, o_ref, *, scale):
    h = pl.program_id(1)
    qi = pl.program_id(2)
    bias = bias_ref[h]
    q = q_ref[...].astype(BF16)
    tri = tri_ref[...]
    row = lax.broadcasted_iota(jnp.int32, (CHUNK, CHUNK), 0)
    col = lax.broadcasted_iota(jnp.int32, (CHUNK, CHUNK), 1)
    valid = col < row

    def block(kb, carry, acc, diagonal):
        start = pl.multiple_of(kb * CHUNK, CHUNK)
        k = k_ref[pl.ds(start, CHUNK), :].astype(BF16)
        v = v_ref[pl.ds(start, CHUNK), :].astype(BF16)
        z = _dot(q, k, NT_DIMS) * scale + bias
        sp = _softplus(z)
        log_fail = -sp
        log_beta = z - sp
        if diagonal:
            log_fail = jnp.where(valid, log_fail, 0.0)
        sums = _dot_split(log_fail, tri, terms=2)
        later = sums[:, :CHUNK] + carry
        w = jnp.exp(log_beta + later)
        if diagonal:
            w = jnp.where(valid, w, 0.0)
        acc = acc + _dot(w.astype(BF16), v)
        carry = carry + sums[:, CHUNK:]
        return carry, acc

    zeros = jnp.zeros((CHUNK, CHUNK), F32)
    carry, acc = block(qi, zeros, zeros, True)

    def body(i, c):
        return block(qi - 1 - i, c[0], c[1], False)

    carry, acc = lax.fori_loop(0, qi, body, (carry, acc))
    o_ref[...] = acc.astype(o_ref.dtype)


def _sb_prompt(proj, bias, bsz, length, heads, q_col, k_col, v_col):
    nq = length // CHUNK
    j = lax.broadcasted_iota(jnp.int32, (CHUNK, CHUNK), 0)
    s = lax.broadcasted_iota(jnp.int32, (CHUNK, CHUNK), 1)
    tri = jnp.concatenate([(j > s).astype(BF16), jnp.ones((CHUNK, CHUNK), BF16)], axis=1)
    return pl.pallas_call(
        functools.partial(_sb_prompt_kernel, scale=CHUNK ** -0.5),
        grid=(bsz, heads, nq),
        in_specs=[pl.BlockSpec(memory_space=pltpu.SMEM),
                  pl.BlockSpec((CHUNK, CHUNK), lambda b, h, i: (b * nq + i, q_col + h)),
                  pl.BlockSpec((length, CHUNK), lambda b, h, i: (b, k_col + h)),
                  pl.BlockSpec((length, CHUNK), lambda b, h, i: (b, v_col + h)),
                  pl.BlockSpec((CHUNK, 2 * CHUNK), lambda b, h, i: (0, 0))],
        out_specs=pl.BlockSpec((CHUNK, CHUNK), lambda b, h, i: (b * nq + i, h)),
        out_shape=jax.ShapeDtypeStruct((bsz * length, heads * CHUNK), BF16),
        compiler_params=_params("parallel", "parallel", "arbitrary"),
        name="sb_prompt",
    )(bias, proj, proj, proj, tri)


def _rotate(x, cos, sin_signed):
    return x * cos + pltpu.roll(x, x.shape[-1] // 2, x.ndim - 1) * sin_signed


def _ret_prompt_kernel(q_ref, k_ref, v_ref, g_ref, cos_ref, sin_ref, dec_ref, te_ref, fs_ref, cd_ref,
                       y_ref, so_ref, s_ref, *, nchunks, kscale):
    c = pl.program_id(2)

    @pl.when(c == 0)
    def _():
        s_ref[...] = jnp.zeros_like(s_ref)

    cos = cos_ref[...]
    sin = sin_ref[...]
    q = _rotate(q_ref[...], cos, sin)
    k = _rotate(k_ref[...], cos, sin) * kscale
    v = v_ref[...].astype(BF16)
    scores = _dot(q.astype(BF16), k.astype(BF16), NT_DIMS) * dec_ref[0]
    out = _dot(scores.astype(BF16), v)
    state = s_ref[...]
    out = out + _dot((q * fs_ref[0]).astype(BF16), state.astype(BF16))
    state = state * cd_ref[0] + _dot((k * te_ref[0]).astype(BF16), v, TN_DIMS)
    s_ref[...] = state
    ms = jnp.mean(out * out, axis=-1, keepdims=True)
    y = out * lax.rsqrt(ms + NORM_EPS) * _silu(g_ref[...])
    y_ref[...] = y.astype(y_ref.dtype)

    @pl.when(c == nchunks - 1)
    def _():
        so_ref[0, 0] = state


def _ret_consts(heads):
    lg = jnp.log(1.0 - 2.0 ** (-5.0 - jnp.arange(heads, dtype=F32)))
    idx = jnp.arange(CHUNK, dtype=F32)
    gap = idx[:, None] - idx[None, :]
    dec = jnp.where(gap >= 0, jnp.exp(lg[:, None, None] * jnp.maximum(gap, 0.0)), 0.0)
    te = jnp.exp(lg[:, None] * (CHUNK - 1.0 - idx)[None, :])[:, :, None]
    fs = jnp.exp(lg[:, None] * (idx + 1.0)[None, :])[:, :, None]
    return lg, dec, te, fs


def _rope_tables(pos, dk):
    half = dk // 2
    inv_freq = ROPE_BASE ** (-jnp.linspace(0.0, 1.0, half, dtype=F32))
    ang = pos.astype(F32)[:, None] * inv_freq[None, :]
    cos, sin = jnp.cos(ang), jnp.sin(ang)
    return jnp.concatenate([cos, cos], axis=-1), jnp.concatenate([-sin, sin], axis=-1)


def _ret_prompt(proj, bsz, length, heads, dk, dv, q_col, k_col, v_col, g_col):
    nc = length // CHUNK
    lg, dec, te, fs = _ret_consts(heads)
    cd = jnp.broadcast_to(jnp.exp(lg * CHUNK)[:, None, None], (heads, 1, dv))
    cos, sin = _rope_tables(jnp.arange(length), dk)
    row = lambda b, h, c: (b * nc + c)
    return pl.pallas_call(
        functools.partial(_ret_prompt_kernel, nchunks=nc, kscale=dk ** -0.5),
        grid=(bsz, heads, nc),
        in_specs=[pl.BlockSpec((CHUNK, dk), lambda b, h, c: (row(b, h, c), q_col + h)),
                  pl.BlockSpec((CHUNK, dk), lambda b, h, c: (row(b, h, c), k_col + h)),
                  pl.BlockSpec((CHUNK, dv), lambda b, h, c: (row(b, h, c), v_col + h)),
                  pl.BlockSpec((CHUNK, dv), lambda b, h, c: (row(b, h, c), g_col + h)),
                  pl.BlockSpec((CHUNK, dk), lambda b, h, c: (c, 0)),
                  pl.BlockSpec((CHUNK, dk), lambda b, h, c: (c, 0)),
                  pl.BlockSpec((1, CHUNK, CHUNK), lambda b, h, c: (h, 0, 0)),
                  pl.BlockSpec((1, CHUNK, 1), lambda b, h, c: (h, 0, 0)),
                  pl.BlockSpec((1, CHUNK, 1), lambda b, h, c: (h, 0, 0)),
                  pl.BlockSpec((1, 1, dv), lambda b, h, c: (h, 0, 0))],
        out_specs=[pl.BlockSpec((CHUNK, dv), lambda b, h, c: (row(b, h, c), h)),
                   pl.BlockSpec((1, 1, dk, dv), lambda b, h, c: (b, h, 0, 0))],
        out_shape=[jax.ShapeDtypeStruct((bsz * length, heads * dv), BF16),
                   jax.ShapeDtypeStruct((bsz, heads, dk, dv), F32)],
        scratch_shapes=[pltpu.VMEM((dk, dv), F32)],
        compiler_params=_params("parallel", "parallel", "arbitrary"),
        name="ret_prompt",
    )(proj, proj, proj, proj, cos, sin, dec, te, fs, cd)


def _conv_taps(x_ref, halo_ref, ext_ref, at_start, width):
    rows = x_ref.shape[0]
    halo = halo_ref[...]
    ext_ref[0:SUBLANES, :] = jnp.where(at_start, jnp.zeros_like(halo), halo)
    ext_ref[SUBLANES:, :] = x_ref[...]
    return [ext_ref[pl.ds(SUBLANES - (width - 1 - t), rows), :] for t in range(width)]


def _conv_silu_kernel(x_ref, halo_ref, w_ref, b_ref, o_ref, ext_ref, *, width, tiles_per_seq):
    at_start = (pl.program_id(0) % tiles_per_seq) == 0
    taps = _conv_taps(x_ref, halo_ref, ext_ref, at_start, width)
    out = b_ref[...] + taps[0] * w_ref[0:1, :]
    for t in range(1, width):
        out = out + taps[t] * w_ref[t:t + 1, :]
    o_ref[...] = _silu(out).astype(o_ref.dtype)


def _conv_gate_kernel(g_ref, u_ref, gh_ref, uh_ref, wg_ref, wu_ref, bg_ref, bu_ref, o_ref, eg_ref, eu_ref,
                      *, width, tiles_per_seq):
    at_start = (pl.program_id(0) % tiles_per_seq) == 0
    tg = _conv_taps(g_ref, gh_ref, eg_ref, at_start, width)
    tu = _conv_taps(u_ref, uh_ref, eu_ref, at_start, width)
    g = bg_ref[...] + tg[0] * wg_ref[0:1, :]
    u = bu_ref[...] + tu[0] * wu_ref[0:1, :]
    for t in range(1, width):
        g = g + tg[t] * wg_ref[t:t + 1, :]
        u = u + tu[t] * wu_ref[t:t + 1, :]
    o_ref[...] = (_silu(g) * u).astype(o_ref.dtype)


def _halo_map(tr):
    step = tr // SUBLANES
    return lambda i: jnp.maximum(i * step - 1, 0)


def _conv_silu_prompt(x, col0, ncols, w, b, length, tr, tc):
    rows = x.shape[0]
    width = w.shape[0]
    c0 = col0 // tc
    hm = _halo_map(tr)
    return pl.pallas_call(
        functools.partial(_conv_silu_kernel, width=width, tiles_per_seq=length // tr),
        grid=(rows // tr, ncols // tc),
        in_specs=[pl.BlockSpec((tr, tc), lambda i, j: (i, c0 + j)),
                  pl.BlockSpec((SUBLANES, tc), lambda i, j: (hm(i), c0 + j)),
                  pl.BlockSpec((width, tc), lambda i, j: (0, j)),
                  pl.BlockSpec((1, tc), lambda i, j: (0, j))],
        out_specs=pl.BlockSpec((tr, tc), lambda i, j: (i, j)),
        out_shape=jax.ShapeDtypeStruct((rows, ncols), F32),
        scratch_shapes=[pltpu.VMEM((tr + SUBLANES, tc), F32)],
        compiler_params=_params("parallel", "parallel"),
        name="conv_silu_prompt",
    )(x, x, w, b.reshape(1, ncols))


def _conv_gate_prompt(hid, w, b, length, tr, tc):
    rows, two_f = hid.shape
    f = two_f // 2
    width = w.shape[0]
    nj = f // tc
    hm = _halo_map(tr)
    b2 = b.reshape(1, two_f)
    return pl.pallas_call(
        functools.partial(_conv_gate_kernel, width=width, tiles_per_seq=length // tr),
        grid=(rows // tr, nj),
        in_specs=[pl.BlockSpec((tr, tc), lambda i, j: (i, j)),
                  pl.BlockSpec((tr, tc), lambda i, j: (i, nj + j)),
                  pl.BlockSpec((SUBLANES, tc), lambda i, j: (hm(i), j)),
                  pl.BlockSpec((SUBLANES, tc), lambda i, j: (hm(i), nj + j)),
                  pl.BlockSpec((width, tc), lambda i, j: (0, j)),
                  pl.BlockSpec((width, tc), lambda i, j: (0, nj + j)),
                  pl.BlockSpec((1, tc), lambda i, j: (0, j)),
                  pl.BlockSpec((1, tc), lambda i, j: (0, nj + j))],
        out_specs=pl.BlockSpec((tr, tc), lambda i, j: (i, j)),
        out_shape=jax.ShapeDtypeStruct((rows, f), BF16),
        scratch_shapes=[pltpu.VMEM((tr + SUBLANES, tc), F32), pltpu.VMEM((tr + SUBLANES, tc), F32)],
        compiler_params=_params("parallel", "parallel"),
        name="conv_gate_prompt",
    )(hid, hid, hid, hid, w, w, b2, b2)


def _ssd_prompt_kernel(x_ref, b_ref, c_ref, z_ref, dtc_ref, dtr_ref, dbc_ref, dbr_ref, alc_ref, alr_ref,
                       dsk_ref, nw_ref, ltri_ref, ones_ref, y_ref, ho_ref, h_ref, *, nchunks, hpg, hdim):
    c = pl.program_id(2)

    @pl.when(c == 0)
    def _():
        h_ref[...] = jnp.zeros_like(h_ref)

    dtc = _softplus(dtc_ref[0] + dbc_ref[0])
    dtr = _softplus(dtr_ref[0] + dbr_ref[0])
    dac = dtc * (-jnp.exp(alc_ref[0]))
    dar = dtr * (-jnp.exp(alr_ref[0]))
    ltri = ltri_ref[...]
    ones = ones_ref[...]
    cumc = _dot_split_lhs01(ltri, dac)
    cumr = _dot_split(dar, ltri, NT_DIMS)
    totc = _dot_split_lhs01(ones, dac)
    totr = _dot_split(dar, ones)

    bmat = b_ref[...].astype(BF16)
    cmat = c_ref[...]
    cb = _dot(cmat.astype(BF16), bmat, NT_DIMS)
    x = x_ref[...]
    row = lax.broadcasted_iota(jnp.int32, (CHUNK, CHUNK), 0)
    col = lax.broadcasted_iota(jnp.int32, (CHUNK, CHUNK), 1)
    causal = col <= row
    pair = LANES // hdim
    ys = []
    for p in range(hpg // pair):
        xp = x[:, p * LANES:(p + 1) * LANES]
        xpb = xp.astype(BF16)
        hp = h_ref[p]
        hpb = hp.astype(BF16)
        yp = None
        xs = None
        cd = None
        for e_local in range(pair):
            e = p * pair + e_local
            cum_col = cumc[:, e:e + 1]
            seg = cum_col - cumr[e:e + 1, :]
            decay = jnp.exp(jnp.where(causal, seg, -jnp.inf))
            w = (cb * decay * dtr[e:e + 1, :]).astype(BF16)
            y_e = _dot(w, xpb) + _dot((cmat * jnp.exp(cum_col)).astype(BF16), hpb, NT_DIMS)
            to_end = jnp.exp(totc[:, e:e + 1] - cum_col) * dtc[:, e:e + 1]
            xs_e = xp * to_end
            cd_e = jnp.broadcast_to(jnp.exp(totr[e:e + 1, :]), (CHUNK, CHUNK))
            if e_local == 0:
                yp, xs, cd = y_e, xs_e, cd_e
            else:
                in_head = (col >= e_local * hdim) & (col < (e_local + 1) * hdim)
                in_rows = (row >= e_local * hdim) & (row < (e_local + 1) * hdim)
                yp = jnp.where(in_head, y_e, yp)
                xs = jnp.where(in_head, xs_e, xs)
                cd = jnp.where(in_rows, cd_e, cd)
        h_ref[p] = hp * cd + _dot(xs.astype(BF16), bmat, TN_DIMS)
        ys.append(yp)
    y = jnp.concatenate(ys, axis=1) + dsk_ref[...] * x
    y = y * _silu(z_ref[...])
    ms = jnp.mean(y * y, axis=-1, keepdims=True)
    y = y * lax.rsqrt(ms + NORM_EPS) * nw_ref[...]
    y_ref[...] = y.astype(y_ref.dtype)

    @pl.when(c == nchunks - 1)
    def _():
        ho_ref[0] = h_ref[...]


def _dot_split_lhs01(m, a, terms=3):
    out = None
    rem = a
    for t in range(terms):
        part = rem.astype(BF16)
        if t + 1 < terms:
            rem = rem - part.astype(F32)
        out_t = _dot(m, part)
        out = out_t if out is None else out + out_t
    return out


def _ssd_prompt(xbc, zproj, dt_raw, dt_bias, a_log, d_skip, norm_w, bsz, length, groups, hpg, hdim, dstate):
    nc = length // CHUNK
    gw = hpg * hdim
    d_inner = groups * gw
    assert dstate == CHUNK and gw % LANES == 0
    heads = groups * hpg
    rows = bsz * length
    dt3 = dt_raw.reshape(bsz, length, groups, hpg)
    dtc = jnp.transpose(dt3, (2, 0, 1, 3)).reshape(groups, rows, hpg)
    dtr = jnp.transpose(dt3, (0, 2, 3, 1)).reshape(bsz * groups, hpg, length)
    i = lax.broadcasted_iota(jnp.int32, (CHUNK, CHUNK), 0)
    j = lax.broadcasted_iota(jnp.int32, (CHUNK, CHUNK), 1)
    ltri = (j <= i).astype(BF16)
    ones = jnp.ones((CHUNK, CHUNK), BF16)
    slabs = gw // LANES
    b_col0 = d_inner // dstate
    c_col0 = b_col0 + groups
    row = lambda b, g, c: b * nc + c
    y, h = pl.pallas_call(
        functools.partial(_ssd_prompt_kernel, nchunks=nc, hpg=hpg, hdim=hdim),
        grid=(bsz, groups, nc),
        in_specs=[pl.BlockSpec((CHUNK, gw), lambda b, g, c: (row(b, g, c), g)),
                  pl.BlockSpec((CHUNK, dstate), lambda b, g, c: (row(b, g, c), b_col0 + g)),
                  pl.BlockSpec((CHUNK, dstate), lambda b, g, c: (row(b, g, c), c_col0 + g)),
                  pl.BlockSpec((CHUNK, gw), lambda b, g, c: (row(b, g, c), g)),
                  pl.BlockSpec((1, CHUNK, hpg), lambda b, g, c: (g, row(b, g, c), 0)),
                  pl.BlockSpec((1, hpg, CHUNK), lambda b, g, c: (b * groups + g, 0, c)),
                  pl.BlockSpec((1, 1, hpg), lambda b, g, c: (g, 0, 0)),
                  pl.BlockSpec((1, hpg, 1), lambda b, g, c: (g, 0, 0)),
                  pl.BlockSpec((1, 1, hpg), lambda b, g, c: (g, 0, 0)),
                  pl.BlockSpec((1, hpg, 1), lambda b, g, c: (g, 0, 0)),
                  pl.BlockSpec((1, gw), lambda b, g, c: (0, g)),
                  pl.BlockSpec((1, gw), lambda b, g, c: (0, g)),
                  pl.BlockSpec((CHUNK, CHUNK), lambda b, g, c: (0, 0)),
                  pl.BlockSpec((CHUNK, CHUNK), lambda b, g, c: (0, 0))],
        out_specs=[pl.BlockSpec((CHUNK, gw), lambda b, g, c: (row(b, g, c), g)),
                   pl.BlockSpec((1, slabs, LANES, dstate), lambda b, g, c: (b, g, 0, 0))],
        out_shape=[jax.ShapeDtypeStruct((rows, d_inner), BF16),
                   jax.ShapeDtypeStruct((bsz, groups * slabs, LANES, dstate), F32)],
        scratch_shapes=[pltpu.VMEM((slabs, LANES, dstate), F32)],
        compiler_params=_params("parallel", "parallel", "arbitrary"),
        name="ssd_prompt",
    )(xbc, xbc, xbc, zproj, dtc, dtr,
      dt_bias.reshape(groups, 1, hpg), dt_bias.reshape(groups, hpg, 1),
      a_log.reshape(groups, 1, hpg), a_log.reshape(groups, hpg, 1),
      jnp.repeat(d_skip, hdim).reshape(1, d_inner), norm_w.reshape(1, d_inner), ltri, ones)
    return y, h.reshape(bsz, heads, hdim, dstate)


def _sb_decode_kernel(pt_ref, q_ref, bias_ref, ones_ref, *refs, pages_per_step, page, scale):
    k_refs = refs[:pages_per_step]
    v_refs = refs[pages_per_step:2 * pages_per_step]
    o_ref, carry_ref, acc_ref = refs[2 * pages_per_step:]
    g = pl.program_id(1)

    @pl.when(g == 0)
    def _():
        carry_ref[...] = jnp.zeros_like(carry_ref)
        acc_ref[...] = jnp.zeros_like(acc_ref)

    q = q_ref[0]
    bias = bias_ref[...]
    ones = ones_ref[...]
    heads, hd = q.shape
    group = SUBLANES
    carry = carry_ref[...]
    acc = acc_ref[...]
    for i in range(pages_per_step):
        k_ref, v_ref = k_refs[i], v_refs[i]

        def chunk(ci, c, k_ref=k_ref, v_ref=v_ref):
            run, acc = c
            t0 = pl.multiple_of((page // group - 1 - ci) * group, group)
            kk = k_ref[0, pl.ds(t0, group)]
            vv = v_ref[0, pl.ds(t0, group)]
            prod = (kk * q[None]).reshape(group * heads, hd)
            zs = _dot(prod.astype(BF16), ones).reshape(group, heads, hd)
            for t in range(group - 1, -1, -1):
                z = zs[t] * scale + bias
                sp = _softplus(z)
                w = jnp.exp(z - sp + run)
                acc = acc + w * vv[t]
                run = run - sp
            return run, acc

        carry, acc = lax.fori_loop(0, page // group, chunk, (carry, acc))
    carry_ref[...] = carry
    acc_ref[...] = acc
    o_ref[0] = acc


def _sb_decode(q, bias, cache_k, cache_v, page_table, pages_per_step):
    bsz, heads, hd = q.shape
    page = cache_k.shape[1]
    n_pages = page_table.shape[1]
    assert n_pages % pages_per_step == 0
    steps = n_pages // pages_per_step

    def page_map(i):
        return lambda b, g, pt: (pt[b, n_pages - 1 - (g * pages_per_step + i)], 0, 0, 0)

    kv_specs = [pl.BlockSpec((1, page, heads, hd), page_map(i)) for i in range(pages_per_step)]
    grid_spec = pltpu.PrefetchScalarGridSpec(
        num_scalar_prefetch=1,
        grid=(bsz, steps),
        in_specs=[pl.BlockSpec((1, heads, hd), lambda b, g, pt: (b, 0, 0)),
                  pl.BlockSpec((heads, hd), lambda b, g, pt: (0, 0)),
                  pl.BlockSpec((hd, hd), lambda b, g, pt: (0, 0))] + kv_specs + kv_specs,
        out_specs=pl.BlockSpec((1, heads, hd), lambda b, g, pt: (b, 0, 0)),
        scratch_shapes=[pltpu.VMEM((heads, hd), F32), pltpu.VMEM((heads, hd), F32)],
    )
    bias_rep = jnp.broadcast_to(bias.astype(F32)[:, None], (heads, hd))
    ones = jnp.ones((hd, hd), BF16)
    return pl.pallas_call(
        functools.partial(_sb_decode_kernel, pages_per_step=pages_per_step, page=page, scale=hd ** -0.5),
        grid_spec=grid_spec,
        out_shape=jax.ShapeDtypeStruct((bsz, heads, hd), F32),
        compiler_params=_params("parallel", "arbitrary"),
        name="sb_decode",
    )(page_table, q, bias_rep, ones, *([cache_k] * pages_per_step), *([cache_v] * pages_per_step))


def _ret_decode_kernel(q_ref, k_ref, v_ref, g_ref, cos_ref, sin_ref, s_ref, y_ref, so_ref, *, gammas, kscale):
    bb, heads, dk = q_ref.shape
    cos = cos_ref[...]
    sin = sin_ref[...]
    pad = jnp.zeros((dk - 2 * heads, dk), F32)
    for bi in range(bb):
        q = _rotate(q_ref[bi], cos, sin)
        k = _rotate(k_ref[bi], cos, sin) * kscale
        qk_t = jnp.concatenate([q, k, pad], axis=0).T
        for h in range(heads):
            q_col = qk_t[:, h:h + 1]
            k_col = qk_t[:, heads + h:heads + h + 1]
            state = s_ref[bi, h] * gammas[h] + k_col * v_ref[bi, h:h + 1, :]
            so_ref[bi, h] = state
            out = jnp.sum(q_col * state, axis=0, keepdims=True)
            ms = jnp.mean(out * out, axis=-1, keepdims=True)
            y_ref[bi, h:h + 1, :] = out * lax.rsqrt(ms + NORM_EPS) * _silu(g_ref[bi, h:h + 1, :])


def _ret_decode(q, k, v, gate, state, pos, bb):
    bsz, heads, dk = q.shape
    dv = v.shape[-1]
    gammas = tuple(1.0 - 2.0 ** (-5.0 - h) for h in range(heads))
    cos, sin = _rope_tables(jnp.full((1,), pos), dk)
    return pl.pallas_call(
        functools.partial(_ret_decode_kernel, gammas=gammas, kscale=dk ** -0.5),
        grid=(bsz // bb,),
        in_specs=[pl.BlockSpec((bb, heads, dk), lambda i: (i, 0, 0)),
                  pl.BlockSpec((bb, heads, dk), lambda i: (i, 0, 0)),
                  pl.BlockSpec((bb, heads, dv), lambda i: (i, 0, 0)),
                  pl.BlockSpec((bb, heads, dv), lambda i: (i, 0, 0)),
                  pl.BlockSpec((1, dk), lambda i: (0, 0)),
                  pl.BlockSpec((1, dk), lambda i: (0, 0)),
                  pl.BlockSpec((bb, heads, dk, dv), lambda i: (i, 0, 0, 0))],
        out_specs=[pl.BlockSpec((bb, heads, dv), lambda i: (i, 0, 0)),
                   pl.BlockSpec((bb, heads, dk, dv), lambda i: (i, 0, 0, 0))],
        out_shape=[jax.ShapeDtypeStruct((bsz, heads, dv), F32),
                   jax.ShapeDtypeStruct((bsz, heads, dk, dv), F32)],
        compiler_params=_params("parallel"),
        name="ret_decode",
    )(q, k, v, gate, cos, sin, state)


def _dt_decode_kernel(dt_ref, bias_ref, alog_ref, dt_out_ref, dec_out_ref):
    dt = _softplus(dt_ref[...] + bias_ref[...])
    dt_out_ref[...] = dt
    dec_out_ref[...] = jnp.exp(dt * (-jnp.exp(alog_ref[...])))


def _dt_decode(dt_raw, dt_bias, a_log):
    bsz, heads = dt_raw.shape
    return pl.pallas_call(
        _dt_decode_kernel,
        out_shape=[jax.ShapeDtypeStruct((bsz, heads), F32), jax.ShapeDtypeStruct((bsz, heads), F32)],
        name="dt_decode",
    )(dt_raw, dt_bias.reshape(1, heads), a_log.reshape(1, heads))


def _ssd_decode_kernel(dec_ref, x_ref, dt_ref, z_ref, b_ref, c_ref, dsk_ref, nw_ref, h_ref, y_ref, ho_ref,
                       *, groups, hpg, hdim):
    bb = x_ref.shape[0]
    gw = hpg * hdim
    dstate = b_ref.shape[-1] // groups
    first = lax.broadcasted_iota(jnp.int32, (SUBLANES, gw), 0) == 0
    first_s = lax.broadcasted_iota(jnp.int32, (SUBLANES, dstate), 0) == 0
    b0 = pl.program_id(0) * bb
    for bi in range(bb):
        for g in range(groups):
            cols = slice(g * gw, (g + 1) * gw)
            scols = slice(g * dstate, (g + 1) * dstate)
            x = x_ref[bi, :, cols]
            dtx = x * dt_ref[bi, :, cols]
            a8 = jnp.where(first, jnp.broadcast_to(dtx, (SUBLANES, gw)), 0.0).astype(BF16)
            b8 = jnp.where(first_s, jnp.broadcast_to(b_ref[bi, :, scols], (SUBLANES, dstate)), 0.0).astype(BF16)
            c8 = jnp.where(first_s, jnp.broadcast_to(c_ref[bi, :, scols], (SUBLANES, dstate)), 0.0).astype(BF16)
            contrib = _dot(a8, b8, TN_DIMS)
            new = []
            for e in range(hpg):
                rows = slice(g * gw + e * hdim, g * gw + (e + 1) * hdim)
                h_new = h_ref[bi, rows, :] * dec_ref[b0 + bi, g * hpg + e] + contrib[e * hdim:(e + 1) * hdim]
                ho_ref[bi, rows, :] = h_new
                new.append(h_new)
            h_g = jnp.concatenate(new, axis=0).astype(BF16)
            y = _dot(c8, h_g, NT_DIMS)[0:1] + dsk_ref[:, cols] * x
            y = y * _silu(z_ref[bi, :, cols])
            ms = jnp.mean(y * y, axis=-1, keepdims=True)
            y_ref[bi, :, cols] = y * lax.rsqrt(ms + NORM_EPS) * nw_ref[:, cols]


def _ssd_decode(xs, bm, cm, z, dt, dec, d_skip, norm_w, state, groups, hpg, hdim, bb):
    bsz, d_inner = xs.shape
    heads = groups * hpg
    dstate = state.shape[-1]
    gs = groups * dstate
    r3 = lambda a: a.reshape(bsz, 1, a.shape[-1])
    dt_rep = jnp.repeat(dt, hdim, axis=1)
    row3 = lambda w: pl.BlockSpec((bb, 1, w), lambda i: (i, 0, 0))
    y, h = pl.pallas_call(
        functools.partial(_ssd_decode_kernel, groups=groups, hpg=hpg, hdim=hdim),
        grid=(bsz // bb,),
        in_specs=[pl.BlockSpec(memory_space=pltpu.SMEM),
                  row3(d_inner), row3(d_inner), row3(d_inner), row3(gs), row3(gs),
                  pl.BlockSpec((1, d_inner), lambda i: (0, 0)),
                  pl.BlockSpec((1, d_inner), lambda i: (0, 0)),
                  pl.BlockSpec((bb, heads * hdim, dstate), lambda i: (i, 0, 0))],
        out_specs=[row3(d_inner),
                   pl.BlockSpec((bb, heads * hdim, dstate), lambda i: (i, 0, 0))],
        out_shape=[jax.ShapeDtypeStruct((bsz, 1, d_inner), F32),
                   jax.ShapeDtypeStruct((bsz, heads * hdim, dstate), F32)],
        compiler_params=_params("parallel"),
        name="ssd_decode",
    )(dec, r3(xs), r3(dt_rep), r3(z), r3(bm), r3(cm),
      jnp.repeat(d_skip, hdim).reshape(1, d_inner), norm_w.reshape(1, d_inner),
      state.reshape(bsz, heads * hdim, dstate))
    return y.reshape(bsz, d_inner), h.reshape(bsz, heads, hdim, dstate)


def _conv_decode_kernel(x_ref, st_ref, w_ref, b_ref, o_ref, *, width, act):
    out = b_ref[...] + x_ref[...] * w_ref[width - 1:width, :]
    for t in range(width - 1):
        out = out + st_ref[t] * w_ref[t:t + 1, :]
    o_ref[...] = _silu(out) if act else out


def _conv_decode(x, state_t, w, b, act, tc):
    bsz, c = x.shape
    width = w.shape[0]
    return pl.pallas_call(
        functools.partial(_conv_decode_kernel, width=width, act=act),
        grid=(c // tc,),
        in_specs=[pl.BlockSpec((bsz, tc), lambda j: (0, j)),
                  pl.BlockSpec((width - 1, bsz, tc), lambda j: (0, 0, j)),
                  pl.BlockSpec((width, tc), lambda j: (0, j)),
                  pl.BlockSpec((1, tc), lambda j: (0, j))],
        out_specs=pl.BlockSpec((bsz, tc), lambda j: (0, j)),
        out_shape=jax.ShapeDtypeStruct((bsz, c), F32),
        compiler_params=_params("parallel"),
        name="conv_decode",
    )(x, state_t, w, b.reshape(1, c))


def _ffn(x_f32, x_bf16, conv_state, w_up, conv_w, conv_b, w_down, ln_g, ln_b, alpha, length):
    rows = x_f32.shape[0]
    two_f = w_up.shape[1]
    f = two_f // 2
    hid = _matmul(x_bf16, w_up, F32, 1024, 1024)
    width = conv_w.shape[0]
    if conv_state is None:
        bsz = rows // length
        act = _conv_gate_prompt(hid, conv_w, conv_b, length, 512, 512)
        new_state = hid.reshape(bsz, length, two_f)[:, length - (width - 1):]
    else:
        conv = _conv_decode(hid, jnp.swapaxes(conv_state, 0, 1), conv_w, conv_b, False, 1024)
        act = (_silu(conv[:, :f]) * conv[:, f:]).astype(BF16)
        new_state = jnp.concatenate([conv_state[:, 1:], hid[:, None]], axis=1)
    xf, xb = _matmul_res_ln(act, w_down, x_f32, ln_g, ln_b, alpha, 512, 512)
    return xf, xb, new_state


def kernel(x_prompt, x_sample, cache_k, cache_v, page_table, state_ret, state_ssm, state_ssm_conv,
           state_ffn_conv, ln1_g, ln1_b, ln2_g, ln2_b, w_in_ab, w_out_ab, sb_bias, w_in_ssd, conv_w_ssd,
           conv_b_ssd, dt_bias, a_log, d_skip, norm_w_ssd, w_out_ssd, w_up, conv_w_ffn, conv_b_ffn, w_down):
    bp, seq, d_model = x_prompt.shape
    bs = x_sample.shape[0]
    depth = ln1_g.shape[0]
    alpha = (2 * depth) ** 0.25
    past_len = page_table.shape[1] * cache_k.shape[2]
    sb_heads, sb_dim = cache_k.shape[3], cache_k.shape[4]
    sb_w = sb_heads * sb_dim
    ret_heads, ret_dk, ret_dv = state_ret.shape[2], state_ret.shape[3], state_ret.shape[4]
    ret_qk_w, ret_v_w = ret_heads * ret_dk, ret_heads * ret_dv
    ssm_heads, ssm_hdim, ssm_dstate = state_ssm.shape[2], state_ssm.shape[3], state_ssm.shape[4]
    d_inner = ssm_heads * ssm_hdim
    conv_dim = state_ssm_conv.shape[-1]
    groups = (conv_dim - d_inner) // (2 * ssm_dstate)
    hpg = ssm_heads // groups
    ssm_width = conv_w_ssd.shape[1]

    xp_f = x_prompt.reshape(bp * seq, d_model)
    xs_f = x_sample.reshape(bs, d_model)
    xp_b = xp_f.astype(BF16)
    xs_b = xs_f.astype(BF16)

    outs = dict(k_p=[], v_p=[], k_s=[], v_s=[], ret_p=[], ret_s=[], ssm_p=[], ssm_s=[],
                sconv_p=[], sconv_s=[], fconv_p=[], fconv_s=[])
    for layer in range(depth):
        idx = layer // 2
        if layer % 2 == 0:
            w_in = w_in_ab[idx].astype(BF16)
            w_out = w_out_ab[idx].astype(BF16)
            q_off, k_off, v_off = 0, sb_w, 2 * sb_w
            rq_off = 3 * sb_w
            rk_off = rq_off + ret_qk_w
            rv_off = rk_off + ret_qk_w
            rg_off = rv_off + ret_v_w
            proj = _matmul(xp_b, w_in, F32, 1024, 1024)
            ya = _sb_prompt(proj, sb_bias[idx], bp, seq, sb_heads,
                            q_off // sb_dim, k_off // sb_dim, v_off // sb_dim)
            yr, s_new = _ret_prompt(proj, bp, seq, ret_heads, ret_dk, ret_dv,
                                    rq_off // ret_dk, rk_off // ret_dk, rv_off // ret_dv, rg_off // ret_dv)
            outs['k_p'].append(proj[:, k_off:k_off + sb_w].reshape(bp, seq, sb_heads, sb_dim))
            outs['v_p'].append(proj[:, v_off:v_off + sb_w].reshape(bp, seq, sb_heads, sb_dim))
            outs['ret_p'].append(s_new)
            merged = jnp.concatenate([ya, yr], axis=-1)
            xp_f, xp_b = _matmul_res_ln(merged, w_out, xp_f, ln1_g[layer], ln1_b[layer], alpha, 512, 1024)
            proj = _matmul(xs_b, w_in, F32, 128, 1024)
            q_s = proj[:, q_off:q_off + sb_w].reshape(bs, sb_heads, sb_dim)
            ya = _sb_decode(q_s, sb_bias[idx], cache_k[idx], cache_v[idx], page_table, 8)
            yr, s_new = _ret_decode(proj[:, rq_off:rq_off + ret_qk_w].reshape(bs, ret_heads, ret_dk),
                                    proj[:, rk_off:rk_off + ret_qk_w].reshape(bs, ret_heads, ret_dk),
                                    proj[:, rv_off:rv_off + ret_v_w].reshape(bs, ret_heads, ret_dv),
                                    proj[:, rg_off:rg_off + ret_v_w].reshape(bs, ret_heads, ret_dv),
                                    state_ret[idx], past_len, 4)
            outs['k_s'].append(proj[:, k_off:k_off + sb_w].reshape(bs, 1, sb_heads, sb_dim))
            outs['v_s'].append(proj[:, v_off:v_off + sb_w].reshape(bs, 1, sb_heads, sb_dim))
            outs['ret_s'].append(s_new)
            merged = jnp.concatenate([ya.reshape(bs, sb_w), yr.reshape(bs, ret_v_w)], axis=-1).astype(BF16)
            xs_f, xs_b = _matmul_res_ln(merged, w_out, xs_f, ln1_g[layer], ln1_b[layer], alpha, 128, 1024)
        else:
            main_w = d_inner + conv_dim
            w_in = w_in_ssd[idx]
            w_main = w_in[:, :main_w].astype(BF16)
            w_dt = w_in[:, main_w:].astype(BF16)
            w_out = w_out_ssd[idx].astype(BF16)
            proj = _matmul(xp_b, w_main, F32, 1024, 1024)
            dt_raw = _matmul(xp_b, w_dt, F32, 1024, ssm_heads)
            xbc = _conv_silu_prompt(proj, d_inner, conv_dim, conv_w_ssd[idx], conv_b_ssd[idx], seq, 512, 512)
            y, h_new = _ssd_prompt(xbc, proj, dt_raw, dt_bias[idx], a_log[idx], d_skip[idx], norm_w_ssd[idx],
                                   bp, seq, groups, hpg, ssm_hdim, ssm_dstate)
            outs['ssm_p'].append(h_new)
            outs['sconv_p'].append(
                proj[:, d_inner:].reshape(bp, seq, conv_dim)[:, seq - (ssm_width - 1):])
            xp_f, xp_b = _matmul_res_ln(y, w_out, xp_f, ln1_g[layer], ln1_b[layer], alpha, 512, 1024)
            proj = _matmul(xs_b, w_main, F32, 128, 1024)
            dt_raw = _matmul(xs_b, w_dt, F32, 128, ssm_heads)
            xbc_raw = proj[:, d_inner:]
            xbc = _conv_decode(xbc_raw, jnp.swapaxes(state_ssm_conv[idx], 0, 1), conv_w_ssd[idx],
                               conv_b_ssd[idx], True, 1024)
            dt, dec = _dt_decode(dt_raw, dt_bias[idx], a_log[idx])
            gs = groups * ssm_dstate
            y, h_new = _ssd_decode(xbc[:, :d_inner], xbc[:, d_inner:d_inner + gs], xbc[:, d_inner + gs:],
                                   proj[:, :d_inner], dt, dec, d_skip[idx], norm_w_ssd[idx], state_ssm[idx],
                                   groups, hpg, ssm_hdim, 2)
            outs['ssm_s'].append(h_new)
            outs['sconv_s'].append(jnp.concatenate([state_ssm_conv[idx][:, 1:], xbc_raw[:, None]], axis=1))
            xs_f, xs_b = _matmul_res_ln(y.astype(BF16), w_out, xs_f, ln1_g[layer], ln1_b[layer], alpha, 128, 1024)

        wu = w_up[layer].astype(BF16)
        wd = w_down[layer].astype(BF16)
        xp_f, xp_b, fc = _ffn(xp_f, xp_b, None, wu, conv_w_ffn[layer], conv_b_ffn[layer], wd,
                              ln2_g[layer], ln2_b[layer], alpha, seq)
        outs['fconv_p'].append(fc)
        xs_f, xs_b, fc = _ffn(xs_f, xs_b, state_ffn_conv[layer], wu, conv_w_ffn[layer], conv_b_ffn[layer], wd,
                              ln2_g[layer], ln2_b[layer], alpha, 1)
        outs['fconv_s'].append(fc)

    st = lambda key: jnp.stack(outs[key])
    return (xp_f.reshape(bp, seq, d_model), xs_f.reshape(bs, 1, d_model),
            st('k_p'), st('v_p'), st('k_s'), st('v_s'), st('ret_p'), st('ret_s'),
            st('ssm_p'), st('ssm_s'), st('sconv_p'), st('sconv_s'), st('fconv_p'), st('fconv_s'))
```

```python
import functools
import math

import jax
import jax.numpy as jnp
from jax import lax
from jax.experimental import pallas as pl
from jax.experimental.pallas import tpu as pltpu

F32 = jnp.float32
BF16 = jnp.bfloat16

LANES = 128
SUBLANES = 8
VMEM_LIMIT_BYTES = 52 * 1024 * 1024

CHUNK = 128
LN_EPS = 1e-5
NORM_EPS = 1e-5
ROPE_BASE = 10000.0

NT_DIMS = (((1,), (1,)), ((), ()))
TN_DIMS = (((0,), (0,)), ((), ()))


def _params(*semantics):
    return pltpu.CompilerParams(dimension_semantics=semantics, vmem_limit_bytes=VMEM_LIMIT_BYTES)


def _softplus(z):
    return jnp.maximum(z, 0.0) + jnp.log(1.0 + jnp.exp(-jnp.abs(z)))


LOG2E = math.log2(math.e)


def _softplus_bits(z2):
    return jnp.maximum(z2, 0.0) + jnp.log2(1.0 + jnp.exp2(-jnp.abs(z2)))


def _silu(x):
    return x * jax.nn.sigmoid(x)


def _dot(a, b, dims=None):
    if dims is None:
        return jnp.dot(a, b, preferred_element_type=F32)
    return lax.dot_general(a, b, dims, preferred_element_type=F32)


def _dot_split(a, m, dims=None, terms=3):
    out = None
    rem = a
    for t in range(terms):
        part = rem.astype(BF16)
        if t + 1 < terms:
            rem = rem - part.astype(F32)
        out_t = _dot(part, m, dims)
        out = out_t if out is None else out + out_t
    return out


def _mm_kernel(x_ref, w_ref, o_ref):
    o_ref[...] = _dot(x_ref[...], w_ref[...]).astype(o_ref.dtype)


def _matmul(x, w, out_dtype, tm, tn, col0=0, ncols=None):
    m, k = x.shape
    n = w.shape[1] - col0 if ncols is None else ncols
    tm, tn = min(tm, m), min(tn, n)
    assert m % tm == 0 and n % tn == 0 and col0 % tn == 0
    c0 = col0 // tn
    return pl.pallas_call(
        _mm_kernel,
        grid=(n // tn, m // tm),
        in_specs=[pl.BlockSpec((tm, k), lambda j, i: (i, 0)),
                  pl.BlockSpec((k, tn), lambda j, i: (0, c0 + j))],
        out_specs=pl.BlockSpec((tm, tn), lambda j, i: (i, j)),
        out_shape=jax.ShapeDtypeStruct((m, n), out_dtype),
        compiler_params=_params("parallel", "parallel"),
        name="matmul",
    )(x, w)


def _mm_res_ln_kernel(*refs, seg_steps, alpha):
    nseg = len(seg_steps)
    a_refs = refs[:nseg]
    w_ref, r_ref, g_ref, b_ref, of_ref, ob_ref, acc_ref = refs[nseg:]
    kk = pl.program_id(1)
    nk = sum(seg_steps)
    tk = w_ref.shape[0]

    def partial_product(a_ref, lo):
        start = pl.multiple_of((kk - lo) * tk, tk)
        return _dot(a_ref[:, pl.ds(start, tk)], w_ref[...])

    @pl.when(kk == 0)
    def _():
        acc_ref[...] = partial_product(a_refs[0], 0)

    lo = 0
    for a_ref, steps in zip(a_refs, seg_steps):
        @pl.when((kk >= max(lo, 1)) & (kk < lo + steps))
        def _(a_ref=a_ref, lo=lo):
            acc_ref[...] += partial_product(a_ref, lo)
        lo += steps

    @pl.when(kk == nk - 1)
    def _():
        y = alpha * r_ref[...] + acc_ref[...]
        mu = jnp.mean(y, axis=-1, keepdims=True)
        d = y - mu
        var = jnp.mean(d * d, axis=-1, keepdims=True)
        out = d * lax.rsqrt(var + LN_EPS) * g_ref[...] + b_ref[...]
        of_ref[...] = out
        ob_ref[...] = out.astype(BF16)


def _matmul_res_ln(acts, w, resid, gain, bias, alpha, tm, tk):
    m = acts[0].shape[0]
    d = w.shape[1]
    tm = min(tm, m)
    assert m % tm == 0 and all(a.shape[1] % tk == 0 for a in acts)
    seg_steps = tuple(a.shape[1] // tk for a in acts)
    nk = sum(seg_steps)
    assert nk * tk == w.shape[0]

    return pl.pallas_call(
        functools.partial(_mm_res_ln_kernel, seg_steps=seg_steps, alpha=alpha),
        grid=(m // tm, nk),
        in_specs=[pl.BlockSpec((tm, a.shape[1]), lambda i, kk: (i, 0)) for a in acts] + [
                  pl.BlockSpec((tk, d), lambda i, kk: (kk, 0)),
                  pl.BlockSpec((tm, d), lambda i, kk: (i, 0)),
                  pl.BlockSpec((1, d), lambda i, kk: (0, 0)),
                  pl.BlockSpec((1, d), lambda i, kk: (0, 0))],
        out_specs=[pl.BlockSpec((tm, d), lambda i, kk: (i, 0)),
                   pl.BlockSpec((tm, d), lambda i, kk: (i, 0))],
        out_shape=[jax.ShapeDtypeStruct((m, d), F32), jax.ShapeDtypeStruct((m, d), BF16)],
        scratch_shapes=[pltpu.VMEM((tm, d), F32)],
        compiler_params=_params("parallel", "arbitrary"),
        name="matmul_res_ln",
    )(*acts, w, resid, gain.reshape(1, d), bias.reshape(1, d))


def _sb_prompt_kernel(bias_ref, q_ref, k_ref, v_ref, tri_ref, o_ref, *, scale, tile):
    h = pl.program_id(1)
    qi = pl.program_id(2)
    bias2 = bias_ref[h] * LOG2E
    q = q_ref[...].astype(BF16)
    tri = tri_ref[...]
    row = lax.broadcasted_iota(jnp.int32, (tile, tile), 0)
    col = lax.broadcasted_iota(jnp.int32, (tile, tile), 1)
    valid = col < row
    nsub = tile // CHUNK

    def group(g, carry, acc, diagonal):
        start = pl.multiple_of(g * tile, tile)
        k = k_ref[pl.ds(start, tile), :].astype(BF16)
        v = v_ref[pl.ds(start, tile), :].astype(BF16)
        z2 = _dot(q, k, NT_DIMS) * (scale * LOG2E) + bias2
        sp = _softplus_bits(z2)
        fail = jnp.where(valid, sp, 0.0) if diagonal else sp
        hi = fail.astype(BF16)
        lo = (fail - hi.astype(F32)).astype(BF16)
        later = [None] * nsub
        for c in range(nsub - 1, -1, -1):
            cols = slice(c * CHUNK, (c + 1) * CHUNK)
            sums = _dot(jnp.concatenate([hi[:, cols], lo[:, cols]], axis=1), tri)
            later[c] = sums[:, :CHUNK] + carry
            carry = carry + sums[:, CHUNK:]
        w = jnp.exp2(z2 - sp + jnp.concatenate(later, axis=1))
        if diagonal:
            w = jnp.where(valid, w, 0.0)
        acc = acc + _dot(w.astype(BF16), v)
        return carry, acc

    zeros = jnp.zeros((tile, CHUNK), F32)
    carry, acc = group(qi, zeros, zeros, True)

    def body(i, c):
        return group(qi - 1 - i, c[0], c[1], False)

    carry, acc = lax.fori_loop(0, qi, body, (carry, acc))
    o_ref[...] = acc.astype(o_ref.dtype)


def _sb_prompt(q, k, v, bias, bsz, length, heads, tile=512):
    tile = min(tile, length)
    nq = length // tile
    j = lax.broadcasted_iota(jnp.int32, (CHUNK, CHUNK), 0)
    s = lax.broadcasted_iota(jnp.int32, (CHUNK, CHUNK), 1)
    tri = -jnp.concatenate([(j > s).astype(BF16), jnp.ones((CHUNK, CHUNK), BF16)], axis=1)
    tri = jnp.concatenate([tri, tri], axis=0)
    return pl.pallas_call(
        functools.partial(_sb_prompt_kernel, scale=CHUNK ** -0.5, tile=tile),
        grid=(bsz, heads, nq),
        in_specs=[pl.BlockSpec(memory_space=pltpu.SMEM),
                  pl.BlockSpec((tile, CHUNK), lambda b, h, i: (b * nq + i, h)),
                  pl.BlockSpec((length, CHUNK), lambda b, h, i: (b, h)),
                  pl.BlockSpec((length, CHUNK), lambda b, h, i: (b, h)),
                  pl.BlockSpec((2 * CHUNK, 2 * CHUNK), lambda b, h, i: (0, 0))],
        out_specs=pl.BlockSpec((tile, CHUNK), lambda b, h, i: (b * nq + i, h)),
        out_shape=jax.ShapeDtypeStruct((bsz * length, heads * CHUNK), BF16),
        compiler_params=_params("parallel", "parallel", "arbitrary"),
        name="sb_prompt",
    )(bias, q, k, v, tri)


def _rotate(x, cos, sin_signed):
    return x * cos + pltpu.roll(x, x.shape[-1] // 2, x.ndim - 1) * sin_signed


def _ret_prompt_kernel(q_ref, k_ref, v_ref, g_ref, cos_ref, sin_ref, dec_ref, te_ref, fs_ref, cd_ref,
                       y_ref, so_ref, s_ref, *, nchunks, kscale):
    c = pl.program_id(2)

    @pl.when(c == 0)
    def _():
        s_ref[...] = jnp.zeros_like(s_ref)

    cos = cos_ref[...]
    sin = sin_ref[...]
    q = _rotate(q_ref[...], cos, sin)
    k = _rotate(k_ref[...], cos, sin) * kscale
    v = v_ref[...].astype(BF16)
    scores = _dot(q.astype(BF16), k.astype(BF16), NT_DIMS) * dec_ref[0]
    out = _dot(scores.astype(BF16), v)
    state = s_ref[...]
    out = out + _dot((q * fs_ref[0]).astype(BF16), state.astype(BF16))
    state = state * cd_ref[0] + _dot((k * te_ref[0]).astype(BF16), v, TN_DIMS)
    s_ref[...] = state
    ms = jnp.mean(out * out, axis=-1, keepdims=True)
    y = out * lax.rsqrt(ms + NORM_EPS) * _silu(g_ref[...])
    y_ref[...] = y.astype(y_ref.dtype)

    @pl.when(c == nchunks - 1)
    def _():
        so_ref[0, 0] = state


def _ret_consts(heads):
    lg = jnp.log(1.0 - 2.0 ** (-5.0 - jnp.arange(heads, dtype=F32)))
    idx = jnp.arange(CHUNK, dtype=F32)
    gap = idx[:, None] - idx[None, :]
    dec = jnp.where(gap >= 0, jnp.exp(lg[:, None, None] * jnp.maximum(gap, 0.0)), 0.0)
    te = jnp.exp(lg[:, None] * (CHUNK - 1.0 - idx)[None, :])[:, :, None]
    fs = jnp.exp(lg[:, None] * (idx + 1.0)[None, :])[:, :, None]
    return lg, dec, te, fs


def _rope_tables(pos, dk):
    half = dk // 2
    inv_freq = ROPE_BASE ** (-jnp.linspace(0.0, 1.0, half, dtype=F32))
    ang = pos.astype(F32)[:, None] * inv_freq[None, :]
    cos, sin = jnp.cos(ang), jnp.sin(ang)
    return jnp.concatenate([cos, cos], axis=-1), jnp.concatenate([-sin, sin], axis=-1)


def _ret_prompt(proj, bsz, length, heads, dk, dv, q_col, k_col, v_col, g_col):
    nc = length // CHUNK
    lg, dec, te, fs = _ret_consts(heads)
    cd = jnp.broadcast_to(jnp.exp(lg * CHUNK)[:, None, None], (heads, 1, dv))
    cos, sin = _rope_tables(jnp.arange(length), dk)
    row = lambda b, h, c: (b * nc + c)
    return pl.pallas_call(
        functools.partial(_ret_prompt_kernel, nchunks=nc, kscale=dk ** -0.5),
        grid=(bsz, heads, nc),
        in_specs=[pl.BlockSpec((CHUNK, dk), lambda b, h, c: (row(b, h, c), q_col + h)),
                  pl.BlockSpec((CHUNK, dk), lambda b, h, c: (row(b, h, c), k_col + h)),
                  pl.BlockSpec((CHUNK, dv), lambda b, h, c: (row(b, h, c), v_col + h)),
                  pl.BlockSpec((CHUNK, dv), lambda b, h, c: (row(b, h, c), g_col + h)),
                  pl.BlockSpec((CHUNK, dk), lambda b, h, c: (c, 0)),
                  pl.BlockSpec((CHUNK, dk), lambda b, h, c: (c, 0)),
                  pl.BlockSpec((1, CHUNK, CHUNK), lambda b, h, c: (h, 0, 0)),
                  pl.BlockSpec((1, CHUNK, 1), lambda b, h, c: (h, 0, 0)),
                  pl.BlockSpec((1, CHUNK, 1), lambda b, h, c: (h, 0, 0)),
                  pl.BlockSpec((1, 1, dv), lambda b, h, c: (h, 0, 0))],
        out_specs=[pl.BlockSpec((CHUNK, dv), lambda b, h, c: (row(b, h, c), h)),
                   pl.BlockSpec((1, 1, dk, dv), lambda b, h, c: (b, h, 0, 0))],
        out_shape=[jax.ShapeDtypeStruct((bsz * length, heads * dv), BF16),
                   jax.ShapeDtypeStruct((bsz, heads, dk, dv), F32)],
        scratch_shapes=[pltpu.VMEM((dk, dv), F32)],
        compiler_params=_params("parallel", "parallel", "arbitrary"),
        name="ret_prompt",
    )(proj, proj, proj, proj, cos, sin, dec, te, fs, cd)


HALO_ROWS = 16


def _mm_conv_kernel(x_ref, xh_ref, *refs, width, tiles_per_seq, nbranch, sub):
    w_refs = refs[0:nbranch]
    cw_refs = refs[nbranch:2 * nbranch]
    cb_refs = refs[2 * nbranch:3 * nbranch]
    o_ref = refs[3 * nbranch]
    tail_refs = refs[3 * nbranch + 1:4 * nbranch + 1]
    ext_refs = refs[4 * nbranch + 1:]
    tm = x_ref.shape[0]
    at_start = (pl.program_id(1) % tiles_per_seq) == 0
    xh = xh_ref[...]
    for br in range(nbranch):
        before = _dot(xh, w_refs[br][...])[HALO_ROWS - SUBLANES:]
        ext_refs[br][0:SUBLANES, :] = jnp.where(at_start, 0.0, before)
    for r in range(tm // sub):
        rows = slice(r * sub, (r + 1) * sub)
        base = SUBLANES + r * sub
        conv = []
        for br in range(nbranch):
            h = _dot(x_ref[rows, :], w_refs[br][...])
            ext_refs[br][base:base + sub, :] = h
            acc = cb_refs[br][...] + h * cw_refs[br][width - 1:width, :]
            for t in range(width - 1):
                back = width - 1 - t
                acc = acc + ext_refs[br][base - back:base - back + sub, :] * cw_refs[br][t:t + 1, :]
            conv.append(acc)
        out = _silu(conv[0]) * conv[1] if nbranch == 2 else _silu(conv[0])
        o_ref[rows, :] = out.astype(o_ref.dtype)
    for br in range(nbranch):
        tail_refs[br][...] = ext_refs[br][tm:tm + SUBLANES, :]


def _matmul_conv(x, w, w_cols, conv_w, conv_b, conv_cols, ncols, length, out_dtype, tm, tn):
    m, k = x.shape
    width = conv_w.shape[0]
    nbranch = len(w_cols)
    assert m % tm == 0 and length % tm == 0 and ncols % tn == 0 and tm % HALO_ROWS == 0
    assert all(c % tn == 0 for c in list(w_cols) + list(conv_cols)) and width - 1 <= SUBLANES
    nj = ncols // tn
    halo_step = tm // HALO_ROWS
    cb2 = conv_b.reshape(1, -1)

    def col_spec(rows, c0):
        return pl.BlockSpec((rows, tn), lambda j, i: (0, c0 // tn + j))

    outs = pl.pallas_call(
        functools.partial(_mm_conv_kernel, width=width, tiles_per_seq=length // tm, nbranch=nbranch,
                          sub=min(256, tm)),
        grid=(nj, m // tm),
        in_specs=[pl.BlockSpec((tm, k), lambda j, i: (i, 0)),
                  pl.BlockSpec((HALO_ROWS, k), lambda j, i: (jnp.maximum(i * halo_step - 1, 0), 0))]
                 + [col_spec(k, c) for c in w_cols]
                 + [col_spec(width, c) for c in conv_cols]
                 + [col_spec(1, c) for c in conv_cols],
        out_specs=[pl.BlockSpec((tm, tn), lambda j, i: (i, j))]
                  + [pl.BlockSpec((SUBLANES, tn), lambda j, i: (i, j))] * nbranch,
        out_shape=[jax.ShapeDtypeStruct((m, ncols), out_dtype)]
                  + [jax.ShapeDtypeStruct((m // tm * SUBLANES, ncols), F32)] * nbranch,
        scratch_shapes=[pltpu.VMEM((tm + SUBLANES, tn), F32)] * nbranch,
        compiler_params=_params("parallel", "arbitrary"),
        name="matmul_conv",
    )(x, x, *([w] * nbranch), *([conv_w] * nbranch), *([cb2] * nbranch))
    return outs[0], outs[1:]


def _tile_tails(tails, bsz, length, tm, keep):
    per_seq = length // tm
    t = tails.reshape(bsz * per_seq, SUBLANES, tails.shape[-1])
    return t[per_seq - 1::per_seq, SUBLANES - keep:, :]


def _ssd_prompt_kernel(x_ref, b_ref, c_ref, z_ref, dtc_ref, dtr_ref, dbc_ref, dbr_ref, alc_ref, alr_ref,
                       dsk_ref, nw_ref, ltri_ref, ones_ref, y_ref, ho_ref, h_ref, *, nchunks, hpg, hdim):
    c = pl.program_id(2)

    @pl.when(c == 0)
    def _():
        h_ref[...] = jnp.zeros_like(h_ref)

    dtc = _softplus(dtc_ref[0] + dbc_ref[0])
    dtr = _softplus(dtr_ref[0] + dbr_ref[0])
    dac = dtc * (-jnp.exp(alc_ref[0]))
    dar = dtr * (-jnp.exp(alr_ref[0]))
    ltri = ltri_ref[...]
    ones = ones_ref[...]
    cumc = _dot_split_lhs01(ltri, dac)
    cumr = _dot_split(dar, ltri, NT_DIMS)
    totc = _dot_split_lhs01(ones, dac)
    totr = _dot_split(dar, ones)

    bmat = b_ref[...].astype(BF16)
    cmat = c_ref[...]
    cb = _dot(cmat.astype(BF16), bmat, NT_DIMS)
    x = x_ref[...]
    row = lax.broadcasted_iota(jnp.int32, (CHUNK, CHUNK), 0)
    col = lax.broadcasted_iota(jnp.int32, (CHUNK, CHUNK), 1)
    causal = col <= row
    pair = LANES // hdim
    ys = []
    for p in range(hpg // pair):
        xp = x[:, p * LANES:(p + 1) * LANES]
        xpb = xp.astype(BF16)
        hp = h_ref[p]
        hpb = hp.astype(BF16)
        yp = None
        xs = None
        cd = None
        for e_local in range(pair):
            e = p * pair + e_local
            cum_col = cumc[:, e:e + 1]
            seg = cum_col - cumr[e:e + 1, :]
            decay = jnp.exp(jnp.where(causal, seg, -jnp.inf))
            w = (cb * decay * dtr[e:e + 1, :]).astype(BF16)
            y_e = _dot(w, xpb) + _dot((cmat * jnp.exp(cum_col)).astype(BF16), hpb, NT_DIMS)
            to_end = jnp.exp(totc[:, e:e + 1] - cum_col) * dtc[:, e:e + 1]
            xs_e = xp * to_end
            cd_e = jnp.broadcast_to(jnp.exp(totr[e:e + 1, :]), (CHUNK, CHUNK))
            if e_local == 0:
                yp, xs, cd = y_e, xs_e, cd_e
            else:
                in_head = (col >= e_local * hdim) & (col < (e_local + 1) * hdim)
                in_rows = (row >= e_local * hdim) & (row < (e_local + 1) * hdim)
                yp = jnp.where(in_head, y_e, yp)
                xs = jnp.where(in_head, xs_e, xs)
                cd = jnp.where(in_rows, cd_e, cd)
        h_ref[p] = hp * cd + _dot(xs.astype(BF16), bmat, TN_DIMS)
        ys.append(yp)
    y = jnp.concatenate(ys, axis=1) + dsk_ref[...] * x
    y = y * _silu(z_ref[...])
    ms = jnp.mean(y * y, axis=-1, keepdims=True)
    y = y * lax.rsqrt(ms + NORM_EPS) * nw_ref[...]
    y_ref[...] = y.astype(y_ref.dtype)

    @pl.when(c == nchunks - 1)
    def _():
        ho_ref[0] = h_ref[...]


def _dot_split_lhs01(m, a, terms=3):
    out = None
    rem = a
    for t in range(terms):
        part = rem.astype(BF16)
        if t + 1 < terms:
            rem = rem - part.astype(F32)
        out_t = _dot(m, part)
        out = out_t if out is None else out + out_t
    return out


def _ssd_prompt(xbc, zproj, dt_raw, dt_bias, a_log, d_skip, norm_w, bsz, length, groups, hpg, hdim, dstate):
    nc = length // CHUNK
    gw = hpg * hdim
    d_inner = groups * gw
    assert dstate == CHUNK and gw % LANES == 0
    heads = groups * hpg
    rows = bsz * length
    dt3 = dt_raw.reshape(bsz, length, groups, hpg)
    dtc = jnp.transpose(dt3, (2, 0, 1, 3)).reshape(groups, rows, hpg)
    dtr = jnp.transpose(dt3, (0, 2, 3, 1)).reshape(bsz * groups, hpg, length)
    i = lax.broadcasted_iota(jnp.int32, (CHUNK, CHUNK), 0)
    j = lax.broadcasted_iota(jnp.int32, (CHUNK, CHUNK), 1)
    ltri = (j <= i).astype(BF16)
    ones = jnp.ones((CHUNK, CHUNK), BF16)
    slabs = gw // LANES
    b_col0 = d_inner // dstate
    c_col0 = b_col0 + groups
    row = lambda b, g, c: b * nc + c
    y, h = pl.pallas_call(
        functools.partial(_ssd_prompt_kernel, nchunks=nc, hpg=hpg, hdim=hdim),
        grid=(bsz, groups, nc),
        in_specs=[pl.BlockSpec((CHUNK, gw), lambda b, g, c: (row(b, g, c), g)),
                  pl.BlockSpec((CHUNK, dstate), lambda b, g, c: (row(b, g, c), b_col0 + g)),
                  pl.BlockSpec((CHUNK, dstate), lambda b, g, c: (row(b, g, c), c_col0 + g)),
                  pl.BlockSpec((CHUNK, gw), lambda b, g, c: (row(b, g, c), g)),
                  pl.BlockSpec((1, CHUNK, hpg), lambda b, g, c: (g, row(b, g, c), 0)),
                  pl.BlockSpec((1, hpg, CHUNK), lambda b, g, c: (b * groups + g, 0, c)),
                  pl.BlockSpec((1, 1, hpg), lambda b, g, c: (g, 0, 0)),
                  pl.BlockSpec((1, hpg, 1), lambda b, g, c: (g, 0, 0)),
                  pl.BlockSpec((1, 1, hpg), lambda b, g, c: (g, 0, 0)),
                  pl.BlockSpec((1, hpg, 1), lambda b, g, c: (g, 0, 0)),
                  pl.BlockSpec((1, gw), lambda b, g, c: (0, g)),
                  pl.BlockSpec((1, gw), lambda b, g, c: (0, g)),
                  pl.BlockSpec((CHUNK, CHUNK), lambda b, g, c: (0, 0)),
                  pl.BlockSpec((CHUNK, CHUNK), lambda b, g, c: (0, 0))],
        out_specs=[pl.BlockSpec((CHUNK, gw), lambda b, g, c: (row(b, g, c), g)),
                   pl.BlockSpec((1, slabs, LANES, dstate), lambda b, g, c: (b, g, 0, 0))],
        out_shape=[jax.ShapeDtypeStruct((rows, d_inner), BF16),
                   jax.ShapeDtypeStruct((bsz, groups * slabs, LANES, dstate), F32)],
        scratch_shapes=[pltpu.VMEM((slabs, LANES, dstate), F32)],
        compiler_params=_params("parallel", "parallel", "arbitrary"),
        name="ssd_prompt",
    )(xbc, xbc, xbc, zproj, dtc, dtr,
      dt_bias.reshape(groups, 1, hpg), dt_bias.reshape(groups, hpg, 1),
      a_log.reshape(groups, 1, hpg), a_log.reshape(groups, hpg, 1),
      jnp.repeat(d_skip, hdim).reshape(1, d_inner), norm_w.reshape(1, d_inner), ltri, ones)
    return y, h.reshape(bsz, heads, hdim, dstate)


def _sb_decode_kernel(pt_ref, q_ref, bias_ref, ones_ref, *refs, pages_per_step, page, scale):
    k_refs = refs[:pages_per_step]
    v_refs = refs[pages_per_step:2 * pages_per_step]
    o_ref, carry_ref, acc_ref = refs[2 * pages_per_step:]
    g = pl.program_id(1)

    @pl.when(g == 0)
    def _():
        carry_ref[...] = jnp.zeros_like(carry_ref)
        acc_ref[...] = jnp.zeros_like(acc_ref)

    q2 = q_ref[0] * (scale * LOG2E)
    bias2 = bias_ref[...]
    ones = ones_ref[...]
    heads, hd = q2.shape
    seg = 16
    nseg = page // seg
    carry = carry_ref[...]
    acc = acc_ref[...]
    for i in range(pages_per_step):
        kk = k_refs[i][0]
        vv = v_refs[i][0]
        prod = (kk * q2[None]).reshape(page * heads, hd)
        z2 = _dot(prod.astype(BF16), ones).reshape(page, heads, hd) + bias2[None]
        sp = _softplus_bits(z2)
        log_beta = z2 - sp
        parts = [None] * nseg
        totals = [None] * nseg
        for s in range(nseg):
            run = None
            part = None
            for t in range((s + 1) * seg - 1, s * seg - 1, -1):
                expo = log_beta[t] if run is None else log_beta[t] + run
                term = jnp.exp2(expo) * vv[t]
                part = term if part is None else part + term
                run = -sp[t] if run is None else run - sp[t]
            parts[s], totals[s] = part, run
        for s in range(nseg - 1, -1, -1):
            acc = acc + jnp.exp2(carry) * parts[s]
            carry = carry + totals[s]
    carry_ref[...] = carry
    acc_ref[...] = acc
    o_ref[0] = acc


def _sb_decode(q, bias, cache_k, cache_v, page_table, pages_per_step):
    bsz, heads, hd = q.shape
    page = cache_k.shape[1]
    n_pages = page_table.shape[1]
    assert n_pages % pages_per_step == 0
    steps = n_pages // pages_per_step

    def page_map(i):
        return lambda b, g, pt: (pt[b, n_pages - 1 - (g * pages_per_step + i)], 0, 0, 0)

    kv_specs = [pl.BlockSpec((1, page, heads, hd), page_map(i)) for i in range(pages_per_step)]
    grid_spec = pltpu.PrefetchScalarGridSpec(
        num_scalar_prefetch=1,
        grid=(bsz, steps),
        in_specs=[pl.BlockSpec((1, heads, hd), lambda b, g, pt: (b, 0, 0)),
                  pl.BlockSpec((heads, hd), lambda b, g, pt: (0, 0)),
                  pl.BlockSpec((hd, hd), lambda b, g, pt: (0, 0))] + kv_specs + kv_specs,
        out_specs=pl.BlockSpec((1, heads, hd), lambda b, g, pt: (b, 0, 0)),
        scratch_shapes=[pltpu.VMEM((heads, hd), F32), pltpu.VMEM((heads, hd), F32)],
    )
    bias_rep = jnp.broadcast_to((bias.astype(F32) * LOG2E)[:, None], (heads, hd))
    ones = jnp.ones((hd, hd), BF16)
    return pl.pallas_call(
        functools.partial(_sb_decode_kernel, pages_per_step=pages_per_step, page=page, scale=hd ** -0.5),
        grid_spec=grid_spec,
        out_shape=jax.ShapeDtypeStruct((bsz, heads, hd), F32),
        compiler_params=_params("parallel", "arbitrary"),
        name="sb_decode",
    )(page_table, q, bias_rep, ones, *([cache_k] * pages_per_step), *([cache_v] * pages_per_step))


def _ret_decode_kernel(q_ref, k_ref, v_ref, g_ref, cos_ref, sin_ref, s_ref, y_ref, so_ref, *, gammas, kscale):
    bb, heads, dk = q_ref.shape
    cos = cos_ref[...]
    sin = sin_ref[...]
    pad = jnp.zeros((dk - 2 * heads, dk), F32)
    for bi in range(bb):
        q = _rotate(q_ref[bi], cos, sin)
        k = _rotate(k_ref[bi], cos, sin) * kscale
        qk_t = jnp.concatenate([q, k, pad], axis=0).T
        for h in range(heads):
            q_col = qk_t[:, h:h + 1]
            k_col = qk_t[:, heads + h:heads + h + 1]
            state = s_ref[bi, h] * gammas[h] + k_col * v_ref[bi, h:h + 1, :]
            so_ref[bi, h] = state
            out = jnp.sum(q_col * state, axis=0, keepdims=True)
            ms = jnp.mean(out * out, axis=-1, keepdims=True)
            y_ref[bi, h:h + 1, :] = out * lax.rsqrt(ms + NORM_EPS) * _silu(g_ref[bi, h:h + 1, :])


def _ret_decode(q, k, v, gate, state, pos, bb):
    bsz, heads, dk = q.shape
    dv = v.shape[-1]
    gammas = tuple(1.0 - 2.0 ** (-5.0 - h) for h in range(heads))
    cos, sin = _rope_tables(jnp.full((1,), pos), dk)
    return pl.pallas_call(
        functools.partial(_ret_decode_kernel, gammas=gammas, kscale=dk ** -0.5),
        grid=(bsz // bb,),
        in_specs=[pl.BlockSpec((bb, heads, dk), lambda i: (i, 0, 0)),
                  pl.BlockSpec((bb, heads, dk), lambda i: (i, 0, 0)),
                  pl.BlockSpec((bb, heads, dv), lambda i: (i, 0, 0)),
                  pl.BlockSpec((bb, heads, dv), lambda i: (i, 0, 0)),
                  pl.BlockSpec((1, dk), lambda i: (0, 0)),
                  pl.BlockSpec((1, dk), lambda i: (0, 0)),
                  pl.BlockSpec((bb, heads, dk, dv), lambda i: (i, 0, 0, 0))],
        out_specs=[pl.BlockSpec((bb, heads, dv), lambda i: (i, 0, 0)),
                   pl.BlockSpec((bb, heads, dk, dv), lambda i: (i, 0, 0, 0))],
        out_shape=[jax.ShapeDtypeStruct((bsz, heads, dv), F32),
                   jax.ShapeDtypeStruct((bsz, heads, dk, dv), F32)],
        compiler_params=_params("parallel"),
        name="ret_decode",
    )(q, k, v, gate, cos, sin, state)


def _dt_decode_kernel(dt_ref, bias_ref, alog_ref, dt_out_ref, dec_out_ref):
    dt = _softplus(dt_ref[...] + bias_ref[...])
    dt_out_ref[...] = dt
    dec_out_ref[...] = jnp.exp(dt * (-jnp.exp(alog_ref[...])))


def _dt_decode(dt_raw, dt_bias, a_log):
    bsz, heads = dt_raw.shape
    return pl.pallas_call(
        _dt_decode_kernel,
        out_shape=[jax.ShapeDtypeStruct((bsz, heads), F32), jax.ShapeDtypeStruct((bsz, heads), F32)],
        name="dt_decode",
    )(dt_raw, dt_bias.reshape(1, heads), a_log.reshape(1, heads))


def _ssd_decode_kernel(dec_ref, x_ref, dt_ref, z_ref, b_ref, c_ref, dsk_ref, nw_ref, h_ref, y_ref, ho_ref,
                       *, groups, hpg, hdim):
    bb = x_ref.shape[0]
    gw = hpg * hdim
    dstate = b_ref.shape[-1] // groups
    first = lax.broadcasted_iota(jnp.int32, (SUBLANES, gw), 0) == 0
    first_s = lax.broadcasted_iota(jnp.int32, (SUBLANES, dstate), 0) == 0
    b0 = pl.program_id(0) * bb
    for bi in range(bb):
        for g in range(groups):
            cols = slice(g * gw, (g + 1) * gw)
            scols = slice(g * dstate, (g + 1) * dstate)
            x = x_ref[bi, :, cols]
            dtx = x * dt_ref[bi, :, cols]
            a8 = jnp.where(first, jnp.broadcast_to(dtx, (SUBLANES, gw)), 0.0).astype(BF16)
            b8 = jnp.where(first_s, jnp.broadcast_to(b_ref[bi, :, scols], (SUBLANES, dstate)), 0.0).astype(BF16)
            c8 = jnp.where(first_s, jnp.broadcast_to(c_ref[bi, :, scols], (SUBLANES, dstate)), 0.0).astype(BF16)
            contrib = _dot(a8, b8, TN_DIMS)
            new = []
            for e in range(hpg):
                rows = slice(g * gw + e * hdim, g * gw + (e + 1) * hdim)
                h_new = h_ref[bi, rows, :] * dec_ref[b0 + bi, g * hpg + e] + contrib[e * hdim:(e + 1) * hdim]
                ho_ref[bi, rows, :] = h_new
                new.append(h_new)
            h_g = jnp.concatenate(new, axis=0).astype(BF16)
            y = _dot(c8, h_g, NT_DIMS)[0:1] + dsk_ref[:, cols] * x
            y = y * _silu(z_ref[bi, :, cols])
            ms = jnp.mean(y * y, axis=-1, keepdims=True)
            y_ref[bi, :, cols] = y * lax.rsqrt(ms + NORM_EPS) * nw_ref[:, cols]


def _ssd_decode(xs, bm, cm, z, dt, dec, d_skip, norm_w, state, groups, hpg, hdim, bb):
    bsz, d_inner = xs.shape
    heads = groups * hpg
    dstate = state.shape[-1]
    gs = groups * dstate
    r3 = lambda a: a.reshape(bsz, 1, a.shape[-1])
    dt_rep = jnp.repeat(dt, hdim, axis=1)
    row3 = lambda w: pl.BlockSpec((bb, 1, w), lambda i: (i, 0, 0))
    y, h = pl.pallas_call(
        functools.partial(_ssd_decode_kernel, groups=groups, hpg=hpg, hdim=hdim),
        grid=(bsz // bb,),
        in_specs=[pl.BlockSpec(memory_space=pltpu.SMEM),
                  row3(d_inner), row3(d_inner), row3(d_inner), row3(gs), row3(gs),
                  pl.BlockSpec((1, d_inner), lambda i: (0, 0)),
                  pl.BlockSpec((1, d_inner), lambda i: (0, 0)),
                  pl.BlockSpec((bb, heads * hdim, dstate), lambda i: (i, 0, 0))],
        out_specs=[row3(d_inner),
                   pl.BlockSpec((bb, heads * hdim, dstate), lambda i: (i, 0, 0))],
        out_shape=[jax.ShapeDtypeStruct((bsz, 1, d_inner), F32),
                   jax.ShapeDtypeStruct((bsz, heads * hdim, dstate), F32)],
        compiler_params=_params("parallel"),
        name="ssd_decode",
    )(dec, r3(xs), r3(dt_rep), r3(z), r3(bm), r3(cm),
      jnp.repeat(d_skip, hdim).reshape(1, d_inner), norm_w.reshape(1, d_inner),
      state.reshape(bsz, heads * hdim, dstate))
    return y.reshape(bsz, d_inner), h.reshape(bsz, heads, hdim, dstate)


def _conv_decode_kernel(*refs, width, nbranch):
    x_refs = refs[0:nbranch]
    st_refs = refs[nbranch:2 * nbranch]
    w_refs = refs[2 * nbranch:3 * nbranch]
    b_refs = refs[3 * nbranch:4 * nbranch]
    o_ref = refs[4 * nbranch]
    conv = []
    for x_ref, st_ref, w_ref, b_ref in zip(x_refs, st_refs, w_refs, b_refs):
        out = b_ref[...] + x_ref[...] * w_ref[width - 1:width, :]
        for t in range(width - 1):
            out = out + st_ref[t] * w_ref[t:t + 1, :]
        conv.append(out)
    out = _silu(conv[0]) * conv[1] if nbranch == 2 else _silu(conv[0])
    o_ref[...] = out.astype(o_ref.dtype)


def _conv_decode(x, state_t, w, b, gated, out_dtype, tc):
    bsz, c = x.shape
    width = w.shape[0]
    nbranch = 2 if gated else 1
    ncols = c // nbranch
    nj = ncols // tc
    offs = [br * nj for br in range(nbranch)]
    b2 = b.reshape(1, c)

    def specs(shape, lead):
        return [pl.BlockSpec(shape, lambda j, o=o: lead + (o + j,)) for o in offs]

    return pl.pallas_call(
        functools.partial(_conv_decode_kernel, width=width, nbranch=nbranch),
        grid=(nj,),
        in_specs=specs((bsz, tc), (0,)) + specs((width - 1, bsz, tc), (0, 0))
                 + specs((width, tc), (0,)) + specs((1, tc), (0,)),
        out_specs=pl.BlockSpec((bsz, tc), lambda j: (0, j)),
        out_shape=jax.ShapeDtypeStruct((bsz, ncols), out_dtype),
        compiler_params=_params("parallel"),
        name="conv_decode",
    )(*([x] * nbranch), *([state_t] * nbranch), *([w] * nbranch), *([b2] * nbranch))


def _ffn(x_f32, x_bf16, conv_state, w_up, conv_w, conv_b, w_down, ln_g, ln_b, alpha, length):
    rows = x_f32.shape[0]
    f = w_up.shape[1] // 2
    width = conv_w.shape[0]
    if conv_state is None:
        tm = 1024
        act, tails = _matmul_conv(x_bf16, w_up, (0, f), conv_w, conv_b, (0, f), f, length, BF16, tm, 512)
        new_state = jnp.concatenate([_tile_tails(t, rows // length, length, tm, width - 1) for t in tails],
                                    axis=-1)
    else:
        hid = _matmul(x_bf16, w_up, F32, 128, 1024)
        act = _conv_decode(hid, jnp.swapaxes(conv_state, 0, 1), conv_w, conv_b, True, BF16, 512)
        new_state = jnp.concatenate([conv_state[:, 1:], hid[:, None]], axis=1)
    xf, xb = _matmul_res_ln([act], w_down, x_f32, ln_g, ln_b, alpha, 512, 512)
    return xf, xb, new_state


def kernel(x_prompt, x_sample, cache_k, cache_v, page_table, state_ret, state_ssm, state_ssm_conv,
           state_ffn_conv, ln1_g, ln1_b, ln2_g, ln2_b, w_in_ab, w_out_ab, sb_bias, w_in_ssd, conv_w_ssd,
           conv_b_ssd, dt_bias, a_log, d_skip, norm_w_ssd, w_out_ssd, w_up, conv_w_ffn, conv_b_ffn, w_down):
    bp, seq, d_model = x_prompt.shape
    bs = x_sample.shape[0]
    depth = ln1_g.shape[0]
    alpha = (2 * depth) ** 0.25
    past_len = page_table.shape[1] * cache_k.shape[2]
    sb_heads, sb_dim = cache_k.shape[3], cache_k.shape[4]
    sb_w = sb_heads * sb_dim
    ret_heads, ret_dk, ret_dv = state_ret.shape[2], state_ret.shape[3], state_ret.shape[4]
    ret_qk_w, ret_v_w = ret_heads * ret_dk, ret_heads * ret_dv
    ssm_heads, ssm_hdim, ssm_dstate = state_ssm.shape[2], state_ssm.shape[3], state_ssm.shape[4]
    d_inner = ssm_heads * ssm_hdim
    conv_dim = state_ssm_conv.shape[-1]
    groups = (conv_dim - d_inner) // (2 * ssm_dstate)
    hpg = ssm_heads // groups
    ssm_width = conv_w_ssd.shape[1]

    xp_f = x_prompt.reshape(bp * seq, d_model)
    xs_f = x_sample.reshape(bs, d_model)
    xp_b = xp_f.astype(BF16)
    xs_b = xs_f.astype(BF16)

    outs = dict(k_p=[], v_p=[], k_s=[], v_s=[], ret_p=[], ret_s=[], ssm_p=[], ssm_s=[],
                sconv_p=[], sconv_s=[], fconv_p=[], fconv_s=[])
    for layer in range(depth):
        idx = layer // 2
        if layer % 2 == 0:
            w_in = w_in_ab[idx].astype(BF16)
            w_out = w_out_ab[idx].astype(BF16)
            q_off, k_off, v_off = 0, sb_w, 2 * sb_w
            rq_off = 3 * sb_w
            rk_off = rq_off + ret_qk_w
            rv_off = rk_off + ret_qk_w
            rg_off = rv_off + ret_v_w
            ret_w = 2 * ret_qk_w + 2 * ret_v_w
            rk_c, rv_c, rg_c = ret_qk_w // ret_dk, 2 * ret_qk_w // ret_dv, (2 * ret_qk_w + ret_v_w) // ret_dv
            q = _matmul(xp_b, w_in, BF16, 1024, 1024, q_off, sb_w)
            k = _matmul(xp_b, w_in, F32, 1024, 1024, k_off, sb_w)
            v = _matmul(xp_b, w_in, F32, 1024, 1024, v_off, sb_w)
            ret = _matmul(xp_b, w_in, F32, 1024, 1024, rq_off, ret_w)
            ya = _sb_prompt(q, k, v, sb_bias[idx], bp, seq, sb_heads)
            yr, s_new = _ret_prompt(ret, bp, seq, ret_heads, ret_dk, ret_dv, 0, rk_c, rv_c, rg_c)
            outs['k_p'].append(k.reshape(bp, seq, sb_heads, sb_dim))
            outs['v_p'].append(v.reshape(bp, seq, sb_heads, sb_dim))
            outs['ret_p'].append(s_new)
            xp_f, xp_b = _matmul_res_ln([ya, yr], w_out, xp_f, ln1_g[layer], ln1_b[layer], alpha, 512, 1024)
            q = _matmul(xs_b, w_in, F32, 128, 1024, q_off, sb_w)
            k = _matmul(xs_b, w_in, F32, 128, 1024, k_off, sb_w)
            v = _matmul(xs_b, w_in, F32, 128, 1024, v_off, sb_w)
            ret = _matmul(xs_b, w_in, F32, 128, 1024, rq_off, ret_w)
            ya = _sb_decode(q.reshape(bs, sb_heads, sb_dim), sb_bias[idx], cache_k[idx], cache_v[idx],
                            page_table, 4)
            yr, s_new = _ret_decode(ret[:, :ret_qk_w].reshape(bs, ret_heads, ret_dk),
                                    ret[:, ret_qk_w:2 * ret_qk_w].reshape(bs, ret_heads, ret_dk),
                                    ret[:, 2 * ret_qk_w:2 * ret_qk_w + ret_v_w].reshape(bs, ret_heads, ret_dv),
                                    ret[:, 2 * ret_qk_w + ret_v_w:].reshape(bs, ret_heads, ret_dv),
                                    state_ret[idx], past_len, 4)
            outs['k_s'].append(k.reshape(bs, 1, sb_heads, sb_dim))
            outs['v_s'].append(v.reshape(bs, 1, sb_heads, sb_dim))
            outs['ret_s'].append(s_new)
            xs_f, xs_b = _matmul_res_ln([ya.reshape(bs, sb_w).astype(BF16), yr.reshape(bs, ret_v_w).astype(BF16)],
                                        w_out, xs_f, ln1_g[layer], ln1_b[layer], alpha, 128, 1024)
        else:
            main_w = d_inner + conv_dim
            w_in = w_in_ssd[idx].astype(BF16)
            w_dt = w_in[:, main_w:]
            w_out = w_out_ssd[idx].astype(BF16)
            gs = groups * ssm_dstate
            tm = 1024
            z = _matmul(xp_b, w_in, F32, 1024, 1024, 0, d_inner)
            dt_raw = _matmul(xp_b, w_dt, F32, 1024, ssm_heads)
            xbc, tails = _matmul_conv(xp_b, w_in, (d_inner,), conv_w_ssd[idx], conv_b_ssd[idx], (0,), conv_dim,
                                      seq, F32, tm, 512)
            y, h_new = _ssd_prompt(xbc, z, dt_raw, dt_bias[idx], a_log[idx], d_skip[idx], norm_w_ssd[idx],
                                   bp, seq, groups, hpg, ssm_hdim, ssm_dstate)
            outs['ssm_p'].append(h_new)
            outs['sconv_p'].append(_tile_tails(tails[0], bp, seq, tm, ssm_width - 1))
            xp_f, xp_b = _matmul_res_ln([y], w_out, xp_f, ln1_g[layer], ln1_b[layer], alpha, 512, 1024)
            z = _matmul(xs_b, w_in, F32, 128, 1024, 0, d_inner)
            dt_raw = _matmul(xs_b, w_dt, F32, 128, ssm_heads)
            xbc_raw = _matmul(xs_b, w_in, F32, 128, 1024, d_inner, conv_dim)
            xbc = _conv_decode(xbc_raw, jnp.swapaxes(state_ssm_conv[idx], 0, 1), conv_w_ssd[idx],
                               conv_b_ssd[idx], False, F32, 1024)
            dt, dec = _dt_decode(dt_raw, dt_bias[idx], a_log[idx])
            y, h_new = _ssd_decode(xbc[:, :d_inner], xbc[:, d_inner:d_inner + gs], xbc[:, d_inner + gs:],
                                   z, dt, dec, d_skip[idx], norm_w_ssd[idx], state_ssm[idx],
                                   groups, hpg, ssm_hdim, 2)
            outs['ssm_s'].append(h_new)
            outs['sconv_s'].append(jnp.concatenate([state_ssm_conv[idx][:, 1:], xbc_raw[:, None]], axis=1))
            xs_f, xs_b = _matmul_res_ln([y.astype(BF16)], w_out, xs_f, ln1_g[layer], ln1_b[layer], alpha,
                                        128, 1024)

        wu = w_up[layer].astype(BF16)
        wd = w_down[layer].astype(BF16)
        xp_f, xp_b, fc = _ffn(xp_f, xp_b, None, wu, conv_w_ffn[layer], conv_b_ffn[layer], wd,
                              ln2_g[layer], ln2_b[layer], alpha, seq)
        outs['fconv_p'].append(fc)
        xs_f, xs_b, fc = _ffn(xs_f, xs_b, state_ffn_conv[layer], wu, conv_w_ffn[layer], conv_b_ffn[layer], wd,
                              ln2_g[layer], ln2_b[layer], alpha, 1)
        outs['fconv_s'].append(fc)

    st = lambda key: jnp.stack(outs[key])
    return (xp_f.reshape(bp, seq, d_model), xs_f.reshape(bs, 1, d_model),
            st('k_p'), st('v_p'), st('k_s'), st('v_s'), st('ret_p'), st('ret_s'),
            st('ssm_p'), st('ssm_s'), st('sconv_p'), st('sconv_s'), st('fconv_p'), st('fconv_s'))
```

```python
import functools
import math

import jax
import jax.numpy as jnp
from jax import lax
from jax.experimental import pallas as pl
from jax.experimental.pallas import tpu as pltpu

F32 = jnp.float32
BF16 = jnp.bfloat16

LANES = 128
SUBLANES = 8
VMEM_LIMIT_BYTES = 52 * 1024 * 1024

CHUNK = 128
LN_EPS = 1e-5
NORM_EPS = 1e-5
ROPE_BASE = 10000.0

NT_DIMS = (((1,), (1,)), ((), ()))
TN_DIMS = (((0,), (0,)), ((), ()))


def _params(*semantics):
    return pltpu.CompilerParams(dimension_semantics=semantics, vmem_limit_bytes=VMEM_LIMIT_BYTES)


def _softplus(z):
    return jnp.maximum(z, 0.0) + jnp.log(1.0 + jnp.exp(-jnp.abs(z)))


LOG2E = math.log2(math.e)


def _softplus_bits(z2):
    return jnp.maximum(z2, 0.0) + jnp.log2(1.0 + jnp.exp2(-jnp.abs(z2)))


def _silu(x):
    return x * jax.nn.sigmoid(x)


def _dot(a, b, dims=None):
    if dims is None:
        return jnp.dot(a, b, preferred_element_type=F32)
    return lax.dot_general(a, b, dims, preferred_element_type=F32)


def _dot_split(a, m, dims=None, terms=3):
    out = None
    rem = a
    for t in range(terms):
        part = rem.astype(BF16)
        if t + 1 < terms:
            rem = rem - part.astype(F32)
        out_t = _dot(part, m, dims)
        out = out_t if out is None else out + out_t
    return out


def _mm_kernel(x_ref, w_ref, o_ref):
    o_ref[...] = _dot(x_ref[...], w_ref[...]).astype(o_ref.dtype)


def _matmul(x, w, out_dtype, tm, tn, col0=0, ncols=None):
    m, k = x.shape
    n = w.shape[1] - col0 if ncols is None else ncols
    tm, tn = min(tm, m), min(tn, n)
    assert m % tm == 0 and n % tn == 0 and col0 % tn == 0
    c0 = col0 // tn
    return pl.pallas_call(
        _mm_kernel,
        grid=(n // tn, m // tm),
        in_specs=[pl.BlockSpec((tm, k), lambda j, i: (i, 0)),
                  pl.BlockSpec((k, tn), lambda j, i: (0, c0 + j))],
        out_specs=pl.BlockSpec((tm, tn), lambda j, i: (i, j)),
        out_shape=jax.ShapeDtypeStruct((m, n), out_dtype),
        compiler_params=_params("parallel", "parallel"),
        name="matmul",
    )(x, w)


def _mm_res_ln_kernel(*refs, widths, alpha):
    nseg = len(widths)
    a_refs = refs[:nseg]
    w_ref, r_ref, g_ref, b_ref, of_ref, ob_ref = refs[nseg:]
    y = alpha * r_ref[...]
    lo = 0
    for a_ref, width in zip(a_refs, widths):
        y = y + _dot(a_ref[...], w_ref[lo:lo + width, :])
        lo += width
    mu = jnp.mean(y, axis=-1, keepdims=True)
    d = y - mu
    var = jnp.mean(d * d, axis=-1, keepdims=True)
    out = d * lax.rsqrt(var + LN_EPS) * g_ref[...] + b_ref[...]
    of_ref[...] = out
    ob_ref[...] = out.astype(BF16)


def _matmul_res_ln(acts, w, resid, gain, bias, alpha, tm):
    m = acts[0].shape[0]
    k, d = w.shape
    tm = min(tm, m)
    widths = tuple(a.shape[1] for a in acts)
    assert m % tm == 0 and sum(widths) == k
    return pl.pallas_call(
        functools.partial(_mm_res_ln_kernel, widths=widths, alpha=alpha),
        grid=(m // tm,),
        in_specs=[pl.BlockSpec((tm, width), lambda i: (i, 0)) for width in widths] + [
                  pl.BlockSpec((k, d), lambda i: (0, 0), pipeline_mode=pl.Buffered(1)),
                  pl.BlockSpec((tm, d), lambda i: (i, 0)),
                  pl.BlockSpec((1, d), lambda i: (0, 0)),
                  pl.BlockSpec((1, d), lambda i: (0, 0))],
        out_specs=[pl.BlockSpec((tm, d), lambda i: (i, 0)),
                   pl.BlockSpec((tm, d), lambda i: (i, 0))],
        out_shape=[jax.ShapeDtypeStruct((m, d), F32), jax.ShapeDtypeStruct((m, d), BF16)],
        compiler_params=_params("parallel"),
        name="matmul_res_ln",
    )(*acts, w, resid, gain.reshape(1, d), bias.reshape(1, d))


def _sb_prompt_kernel(bias_ref, q_ref, k_ref, v_ref, tri_ref, o_ref, *, scale, tile):
    h = pl.program_id(1)
    qi = pl.program_id(2)
    bias2 = bias_ref[h] * LOG2E
    q = q_ref[...].astype(BF16)
    tri = tri_ref[...]
    row = lax.broadcasted_iota(jnp.int32, (tile, tile), 0)
    col = lax.broadcasted_iota(jnp.int32, (tile, tile), 1)
    valid = col < row
    nsub = tile // CHUNK

    def group(g, carry, acc, diagonal):
        start = pl.multiple_of(g * tile, tile)
        k = k_ref[pl.ds(start, tile), :].astype(BF16)
        v = v_ref[pl.ds(start, tile), :].astype(BF16)
        z2 = _dot(q, k, NT_DIMS) * (scale * LOG2E) + bias2
        sp = _softplus_bits(z2)
        fail = jnp.where(valid, sp, 0.0) if diagonal else sp
        hi = fail.astype(BF16)
        lo = (fail - hi.astype(F32)).astype(BF16)
        later = [None] * nsub
        for c in range(nsub - 1, -1, -1):
            cols = slice(c * CHUNK, (c + 1) * CHUNK)
            sums = _dot(jnp.concatenate([hi[:, cols], lo[:, cols]], axis=1), tri)
            later[c] = sums[:, :CHUNK] + carry
            carry = carry + sums[:, CHUNK:]
        w = jnp.exp2(z2 - sp + jnp.concatenate(later, axis=1))
        if diagonal:
            w = jnp.where(valid, w, 0.0)
        acc = acc + _dot(w.astype(BF16), v)
        return carry, acc

    zeros = jnp.zeros((tile, CHUNK), F32)
    carry, acc = group(qi, zeros, zeros, True)

    def body(i, c):
        return group(qi - 1 - i, c[0], c[1], False)

    carry, acc = lax.fori_loop(0, qi, body, (carry, acc))
    o_ref[...] = acc.astype(o_ref.dtype)


def _sb_prompt(q, k, v, bias, bsz, length, heads, tile=512):
    tile = min(tile, length)
    nq = length // tile
    j = lax.broadcasted_iota(jnp.int32, (CHUNK, CHUNK), 0)
    s = lax.broadcasted_iota(jnp.int32, (CHUNK, CHUNK), 1)
    tri = -jnp.concatenate([(j > s).astype(BF16), jnp.ones((CHUNK, CHUNK), BF16)], axis=1)
    tri = jnp.concatenate([tri, tri], axis=0)
    return pl.pallas_call(
        functools.partial(_sb_prompt_kernel, scale=CHUNK ** -0.5, tile=tile),
        grid=(bsz, heads, nq),
        in_specs=[pl.BlockSpec(memory_space=pltpu.SMEM),
                  pl.BlockSpec((tile, CHUNK), lambda b, h, i: (b * nq + i, h)),
                  pl.BlockSpec((length, CHUNK), lambda b, h, i: (b, h)),
                  pl.BlockSpec((length, CHUNK), lambda b, h, i: (b, h)),
                  pl.BlockSpec((2 * CHUNK, 2 * CHUNK), lambda b, h, i: (0, 0))],
        out_specs=pl.BlockSpec((tile, CHUNK), lambda b, h, i: (b * nq + i, h)),
        out_shape=jax.ShapeDtypeStruct((bsz * length, heads * CHUNK), BF16),
        compiler_params=_params("parallel", "parallel", "arbitrary"),
        name="sb_prompt",
    )(bias, q, k, v, tri)


def _rotate(x, cos, sin_signed):
    return x * cos + pltpu.roll(x, x.shape[-1] // 2, x.ndim - 1) * sin_signed


def _ret_prompt_kernel(q_ref, k_ref, v_ref, g_ref, cos_ref, sin_ref, dec_ref, te_ref, fs_ref, cd_ref,
                       y_ref, so_ref, s_ref, *, nchunks, kscale, dk, dv):
    c = pl.program_id(2)

    @pl.when(c == 0)
    def _():
        s_ref[...] = jnp.zeros_like(s_ref)

    cos = cos_ref[...]
    sin = sin_ref[...]
    for h in range(s_ref.shape[0]):
        qk = slice(h * dk, (h + 1) * dk)
        vg = slice(h * dv, (h + 1) * dv)
        q = _rotate(q_ref[:, qk], cos, sin)
        k = _rotate(k_ref[:, qk], cos, sin) * kscale
        v = v_ref[:, vg].astype(BF16)
        scores = _dot(q.astype(BF16), k.astype(BF16), NT_DIMS) * dec_ref[h]
        out = _dot(scores.astype(BF16), v)
        state = s_ref[h]
        out = out + _dot((q * fs_ref[h]).astype(BF16), state.astype(BF16))
        state = state * cd_ref[h] + _dot((k * te_ref[h]).astype(BF16), v, TN_DIMS)
        s_ref[h] = state
        ms = jnp.mean(out * out, axis=-1, keepdims=True)
        y = out * lax.rsqrt(ms + NORM_EPS) * _silu(g_ref[:, vg])
        y_ref[:, vg] = y.astype(y_ref.dtype)

    @pl.when(c == nchunks - 1)
    def _():
        so_ref[0] = s_ref[...]


def _ret_consts(heads, dk):
    lg = jnp.log(1.0 - 2.0 ** (-5.0 - jnp.arange(heads, dtype=F32)))
    idx = jnp.arange(CHUNK, dtype=F32)
    gap = idx[:, None] - idx[None, :]
    dec = jnp.where(gap >= 0, jnp.exp(lg[:, None, None] * jnp.maximum(gap, 0.0)), 0.0)
    te = jnp.exp(lg[:, None] * (CHUNK - 1.0 - idx)[None, :])
    fs = jnp.exp(lg[:, None] * (idx + 1.0)[None, :])
    rep = lambda a: jnp.broadcast_to(a[:, :, None], (heads, CHUNK, dk))
    return lg, dec, rep(te), rep(fs)


def _rope_tables(pos, dk):
    half = dk // 2
    inv_freq = ROPE_BASE ** (-jnp.linspace(0.0, 1.0, half, dtype=F32))
    ang = pos.astype(F32)[:, None] * inv_freq[None, :]
    cos, sin = jnp.cos(ang), jnp.sin(ang)
    return jnp.concatenate([cos, cos], axis=-1), jnp.concatenate([-sin, sin], axis=-1)


def _ret_prompt(proj, bsz, length, heads, dk, dv, q_col, k_col, v_col, g_col, hb=4):
    nc = length // CHUNK
    assert heads % hb == 0 and all(c % (hb * dk) == 0 for c in (q_col, k_col))
    assert all(c % (hb * dv) == 0 for c in (v_col, g_col))
    lg, dec, te, fs = _ret_consts(heads, dk)
    cd = jnp.broadcast_to(jnp.exp(lg * CHUNK)[:, None, None], (heads, 1, dv))
    cos, sin = _rope_tables(jnp.arange(length), dk)
    row = lambda b, c: b * nc + c
    qk_spec = lambda col: pl.BlockSpec((CHUNK, hb * dk), lambda b, h, c: (row(b, c), col // (hb * dk) + h))
    vg_spec = lambda col: pl.BlockSpec((CHUNK, hb * dv), lambda b, h, c: (row(b, c), col // (hb * dv) + h))
    head_spec = lambda r, w: pl.BlockSpec((hb, r, w), lambda b, h, c: (h, 0, 0))
    return pl.pallas_call(
        functools.partial(_ret_prompt_kernel, nchunks=nc, kscale=dk ** -0.5, dk=dk, dv=dv),
        grid=(bsz, heads // hb, nc),
        in_specs=[qk_spec(q_col), qk_spec(k_col), vg_spec(v_col), vg_spec(g_col),
                  pl.BlockSpec((CHUNK, dk), lambda b, h, c: (c, 0)),
                  pl.BlockSpec((CHUNK, dk), lambda b, h, c: (c, 0)),
                  head_spec(CHUNK, CHUNK), head_spec(CHUNK, dk), head_spec(CHUNK, dk), head_spec(1, dv)],
        out_specs=[pl.BlockSpec((CHUNK, hb * dv), lambda b, h, c: (row(b, c), h)),
                   pl.BlockSpec((1, hb, dk, dv), lambda b, h, c: (b, h, 0, 0))],
        out_shape=[jax.ShapeDtypeStruct((bsz * length, heads * dv), BF16),
                   jax.ShapeDtypeStruct((bsz, heads, dk, dv), F32)],
        scratch_shapes=[pltpu.VMEM((hb, dk, dv), F32)],
        compiler_params=_params("parallel", "parallel", "arbitrary"),
        name="ret_prompt",
    )(proj, proj, proj, proj, cos, sin, dec, te, fs, cd)


HALO_ROWS = 16
RES_LN_ROWS = 256
CONV_ROWS = 2048
CONV_SUB_ROWS = 1024


def _mm_conv_kernel(x_ref, xh_ref, *refs, width, tiles_per_seq, nbranch, sub):
    w_refs = refs[0:nbranch]
    cw_refs = refs[nbranch:2 * nbranch]
    cb_refs = refs[2 * nbranch:3 * nbranch]
    o_ref = refs[3 * nbranch]
    tail_refs = refs[3 * nbranch + 1:4 * nbranch + 1]
    ext_refs = refs[4 * nbranch + 1:]
    tm = x_ref.shape[0]
    at_start = (pl.program_id(1) % tiles_per_seq) == 0
    xh = xh_ref[...]
    for br in range(nbranch):
        before = _dot(xh, w_refs[br][...])[HALO_ROWS - SUBLANES:]
        ext_refs[br][0:SUBLANES, :] = jnp.where(at_start, 0.0, before)
    for r in range(tm // sub):
        rows = slice(r * sub, (r + 1) * sub)
        base = SUBLANES + r * sub
        conv = []
        for br in range(nbranch):
            h = _dot(x_ref[rows, :], w_refs[br][...])
            ext_refs[br][base:base + sub, :] = h
            acc = cb_refs[br][...] + h * cw_refs[br][width - 1:width, :]
            for t in range(width - 1):
                back = width - 1 - t
                acc = acc + ext_refs[br][base - back:base - back + sub, :] * cw_refs[br][t:t + 1, :]
            conv.append(acc)
        out = _silu(conv[0]) * conv[1] if nbranch == 2 else _silu(conv[0])
        o_ref[rows, :] = out.astype(o_ref.dtype)
    for br in range(nbranch):
        tail_refs[br][...] = ext_refs[br][tm:tm + SUBLANES, :]


def _matmul_conv(x, w, w_cols, conv_w, conv_b, conv_cols, ncols, length, out_dtype, tm, tn, sub):
    m, k = x.shape
    width = conv_w.shape[0]
    nbranch = len(w_cols)
    assert m % tm == 0 and length % tm == 0 and ncols % tn == 0 and tm % HALO_ROWS == 0
    assert all(c % tn == 0 for c in list(w_cols) + list(conv_cols)) and width - 1 <= SUBLANES
    nj = ncols // tn
    halo_step = tm // HALO_ROWS
    cb2 = conv_b.reshape(1, -1)

    def col_spec(rows, c0):
        return pl.BlockSpec((rows, tn), lambda j, i: (0, c0 // tn + j))

    outs = pl.pallas_call(
        functools.partial(_mm_conv_kernel, width=width, tiles_per_seq=length // tm, nbranch=nbranch,
                          sub=min(sub, tm)),
        grid=(nj, m // tm),
        in_specs=[pl.BlockSpec((tm, k), lambda j, i: (i, 0)),
                  pl.BlockSpec((HALO_ROWS, k), lambda j, i: (jnp.maximum(i * halo_step - 1, 0), 0))]
                 + [col_spec(k, c) for c in w_cols]
                 + [col_spec(width, c) for c in conv_cols]
                 + [col_spec(1, c) for c in conv_cols],
        out_specs=[pl.BlockSpec((tm, tn), lambda j, i: (i, j))]
                  + [pl.BlockSpec((SUBLANES, tn), lambda j, i: (i, j))] * nbranch,
        out_shape=[jax.ShapeDtypeStruct((m, ncols), out_dtype)]
                  + [jax.ShapeDtypeStruct((m // tm * SUBLANES, ncols), F32)] * nbranch,
        scratch_shapes=[pltpu.VMEM((tm + SUBLANES, tn), F32)] * nbranch,
        compiler_params=_params("parallel", "arbitrary"),
        name="matmul_conv",
    )(x, x, *([w] * nbranch), *([conv_w] * nbranch), *([cb2] * nbranch))
    return outs[0], outs[1:]


def _tile_tails(tails, bsz, length, tm, keep):
    per_seq = length // tm
    t = tails.reshape(bsz * per_seq, SUBLANES, tails.shape[-1])
    return t[per_seq - 1::per_seq, SUBLANES - keep:, :]


def _ssd_prompt_kernel(x_ref, b_ref, c_ref, z_ref, dtc_ref, dtr_ref, dbc_ref, dbr_ref, alc_ref, alr_ref,
                       dsk_ref, nw_ref, ltri_ref, ones_ref, y_ref, ho_ref, h_ref, *, nchunks, hpg, hdim):
    c = pl.program_id(2)

    @pl.when(c == 0)
    def _():
        h_ref[...] = jnp.zeros_like(h_ref)

    ltri = ltri_ref[...]
    ones = ones_ref[...]
    row = lax.broadcasted_iota(jnp.int32, (CHUNK, CHUNK), 0)
    col = lax.broadcasted_iota(jnp.int32, (CHUNK, CHUNK), 1)
    causal = col <= row
    pair = LANES // hdim
    slabs = hpg // pair
    gw = hpg * hdim
    dstate = b_ref.shape[1] // dtc_ref.shape[0]
    for gi in range(dtc_ref.shape[0]):
        gcols = slice(gi * gw, (gi + 1) * gw)
        scols = slice(gi * dstate, (gi + 1) * dstate)
        dtc = _softplus(dtc_ref[gi] + dbc_ref[gi])
        dtr = _softplus(dtr_ref[gi] + dbr_ref[gi])
        dac = dtc * (-jnp.exp(alc_ref[gi]))
        dar = dtr * (-jnp.exp(alr_ref[gi]))
        cumc = _dot_split_lhs01(ltri, dac)
        cumr = _dot_split(dar, ltri, NT_DIMS)
        totr = _dot_split(dar, ones)
        to_end = jnp.exp(totr - cumr) * dtr
        chunk_decay = jnp.exp(totr)

        bmat = b_ref[:, scols].astype(BF16)
        cmat = c_ref[:, scols]
        cb = _dot(cmat.astype(BF16), bmat, NT_DIMS)
        x = x_ref[:, gcols]
        ys = []
        for p in range(slabs):
            xp = x[:, p * LANES:(p + 1) * LANES]
            xpb = xp.astype(BF16)
            hp = h_ref[gi * slabs + p]
            hpb = hp.astype(BF16)
            yp = None
            te = None
            cd = None
            for e_local in range(pair):
                e = p * pair + e_local
                cum_i = jnp.broadcast_to(cumc[:, e:e + 1], (CHUNK, CHUNK))
                seg = cum_i - cumr[e:e + 1, :]
                decay = jnp.exp(jnp.where(causal, seg, -jnp.inf))
                w = (cb * decay * dtr[e:e + 1, :]).astype(BF16)
                y_e = _dot(w, xpb) + _dot((cmat * jnp.exp(cum_i)).astype(BF16), hpb, NT_DIMS)
                te_e = jnp.broadcast_to(to_end[e:e + 1, :], (CHUNK, CHUNK))
                cd_e = jnp.broadcast_to(chunk_decay[e:e + 1, :], (CHUNK, CHUNK))
                if e_local == 0:
                    yp, te, cd = y_e, te_e, cd_e
                else:
                    in_head = (col >= e_local * hdim) & (col < (e_local + 1) * hdim)
                    in_rows = (row >= e_local * hdim) & (row < (e_local + 1) * hdim)
                    yp = jnp.where(in_head, y_e, yp)
                    te = jnp.where(in_rows, te_e, te)
                    cd = jnp.where(in_rows, cd_e, cd)
            h_ref[gi * slabs + p] = hp * cd + _dot((xp.T * te).astype(BF16), bmat)
            ys.append(yp)
        y = jnp.concatenate(ys, axis=1) + dsk_ref[:, gcols] * x
        y = y * _silu(z_ref[:, gcols])
        ms = jnp.mean(y * y, axis=-1, keepdims=True)
        y = y * lax.rsqrt(ms + NORM_EPS) * nw_ref[:, gcols]
        y_ref[:, gcols] = y.astype(y_ref.dtype)

    @pl.when(c == nchunks - 1)
    def _():
        ho_ref[0] = h_ref[...]


def _dot_split_lhs01(m, a, terms=3):
    out = None
    rem = a
    for t in range(terms):
        part = rem.astype(BF16)
        if t + 1 < terms:
            rem = rem - part.astype(F32)
        out_t = _dot(m, part)
        out = out_t if out is None else out + out_t
    return out


def _ssd_prompt(xbc, zproj, dt_raw, dt_bias, a_log, d_skip, norm_w, bsz, length, groups, hpg, hdim, dstate, gb=4):
    nc = length // CHUNK
    gw = hpg * hdim
    d_inner = groups * gw
    assert dstate == CHUNK and gw % LANES == 0 and groups % gb == 0 and d_inner % (gb * dstate) == 0
    heads = groups * hpg
    rows = bsz * length
    dt3 = dt_raw.reshape(bsz, length, groups, hpg)
    dtc = jnp.transpose(dt3, (2, 0, 1, 3)).reshape(groups, rows, hpg)
    dtr = jnp.transpose(dt3, (0, 2, 3, 1)).reshape(bsz * groups, hpg, length)
    i = lax.broadcasted_iota(jnp.int32, (CHUNK, CHUNK), 0)
    j = lax.broadcasted_iota(jnp.int32, (CHUNK, CHUNK), 1)
    ltri = (j <= i).astype(BF16)
    ones = jnp.ones((CHUNK, CHUNK), BF16)
    slabs = gw // LANES
    ng = groups // gb
    b_blk0 = d_inner // (gb * dstate)
    c_blk0 = b_blk0 + ng
    row = lambda b, c: b * nc + c
    wide = pl.BlockSpec((CHUNK, gb * gw), lambda b, g, c: (row(b, c), g))
    par = lambda r, w: pl.BlockSpec((gb, r, w), lambda b, g, c: (g, 0, 0))
    vec = pl.BlockSpec((1, gb * gw), lambda b, g, c: (0, g))
    const = pl.BlockSpec((CHUNK, CHUNK), lambda b, g, c: (0, 0))
    y, h = pl.pallas_call(
        functools.partial(_ssd_prompt_kernel, nchunks=nc, hpg=hpg, hdim=hdim),
        grid=(bsz, ng, nc),
        in_specs=[wide,
                  pl.BlockSpec((CHUNK, gb * dstate), lambda b, g, c: (row(b, c), b_blk0 + g)),
                  pl.BlockSpec((CHUNK, gb * dstate), lambda b, g, c: (row(b, c), c_blk0 + g)),
                  wide,
                  pl.BlockSpec((gb, CHUNK, hpg), lambda b, g, c: (g, row(b, c), 0)),
                  pl.BlockSpec((gb, hpg, CHUNK), lambda b, g, c: (b * ng + g, 0, c)),
                  par(1, hpg), par(hpg, 1), par(1, hpg), par(hpg, 1), vec, vec, const, const],
        out_specs=[wide,
                   pl.BlockSpec((1, gb * slabs, LANES, dstate), lambda b, g, c: (b, g, 0, 0))],
        out_shape=[jax.ShapeDtypeStruct((rows, d_inner), BF16),
                   jax.ShapeDtypeStruct((bsz, groups * slabs, LANES, dstate), F32)],
        scratch_shapes=[pltpu.VMEM((gb * slabs, LANES, dstate), F32)],
        compiler_params=_params("parallel", "parallel", "arbitrary"),
        name="ssd_prompt",
    )(xbc, xbc, xbc, zproj, dtc, dtr,
      dt_bias.reshape(groups, 1, hpg), dt_bias.reshape(groups, hpg, 1),
      a_log.reshape(groups, 1, hpg), a_log.reshape(groups, hpg, 1),
      jnp.repeat(d_skip, hdim).reshape(1, d_inner), norm_w.reshape(1, d_inner), ltri, ones)
    return y, h.reshape(bsz, heads, hdim, dstate)


def _sb_decode_kernel(pt_ref, q_ref, bias_ref, ones_ref, *refs, pages_per_step, page, scale):
    k_refs = refs[:pages_per_step]
    v_refs = refs[pages_per_step:2 * pages_per_step]
    o_ref, carry_ref, acc_ref = refs[2 * pages_per_step:]
    g = pl.program_id(1)

    @pl.when(g == 0)
    def _():
        carry_ref[...] = jnp.zeros_like(carry_ref)
        acc_ref[...] = jnp.zeros_like(acc_ref)

    q2 = q_ref[0] * (scale * LOG2E)
    bias2 = bias_ref[...]
    ones = ones_ref[...]
    heads, hd = q2.shape
    seg = 16
    nseg = page // seg
    carry = carry_ref[...]
    acc = acc_ref[...]
    for i in range(pages_per_step):
        kk = k_refs[i][0]
        vv = v_refs[i][0]
        prod = (kk * q2[None]).reshape(page * heads, hd)
        z2 = _dot(prod.astype(BF16), ones).reshape(page, heads, hd) + bias2[None]
        sp = _softplus_bits(z2)
        log_beta = z2 - sp
        parts = [None] * nseg
        totals = [None] * nseg
        for s in range(nseg):
            run = None
            part = None
            for t in range((s + 1) * seg - 1, s * seg - 1, -1):
                expo = log_beta[t] if run is None else log_beta[t] + run
                term = jnp.exp2(expo) * vv[t]
                part = term if part is None else part + term
                run = -sp[t] if run is None else run - sp[t]
            parts[s], totals[s] = part, run
        for s in range(nseg - 1, -1, -1):
            acc = acc + jnp.exp2(carry) * parts[s]
            carry = carry + totals[s]
    carry_ref[...] = carry
    acc_ref[...] = acc
    o_ref[0] = acc


def _sb_decode(q, bias, cache_k, cache_v, page_table, pages_per_step):
    bsz, heads, hd = q.shape
    page = cache_k.shape[1]
    n_pages = page_table.shape[1]
    assert n_pages % pages_per_step == 0
    steps = n_pages // pages_per_step

    def page_map(i):
        return lambda b, g, pt: (pt[b, n_pages - 1 - (g * pages_per_step + i)], 0, 0, 0)

    kv_specs = [pl.BlockSpec((1, page, heads, hd), page_map(i)) for i in range(pages_per_step)]
    grid_spec = pltpu.PrefetchScalarGridSpec(
        num_scalar_prefetch=1,
        grid=(bsz, steps),
        in_specs=[pl.BlockSpec((1, heads, hd), lambda b, g, pt: (b, 0, 0)),
                  pl.BlockSpec((heads, hd), lambda b, g, pt: (0, 0)),
                  pl.BlockSpec((hd, hd), lambda b, g, pt: (0, 0))] + kv_specs + kv_specs,
        out_specs=pl.BlockSpec((1, heads, hd), lambda b, g, pt: (b, 0, 0)),
        scratch_shapes=[pltpu.VMEM((heads, hd), F32), pltpu.VMEM((heads, hd), F32)],
    )
    bias_rep = jnp.broadcast_to((bias.astype(F32) * LOG2E)[:, None], (heads, hd))
    ones = jnp.ones((hd, hd), BF16)
    return pl.pallas_call(
        functools.partial(_sb_decode_kernel, pages_per_step=pages_per_step, page=page, scale=hd ** -0.5),
        grid_spec=grid_spec,
        out_shape=jax.ShapeDtypeStruct((bsz, heads, hd), F32),
        compiler_params=_params("parallel", "arbitrary"),
        name="sb_decode",
    )(page_table, q, bias_rep, ones, *([cache_k] * pages_per_step), *([cache_v] * pages_per_step))


def _ret_decode_kernel(q_ref, k_ref, v_ref, g_ref, cos_ref, sin_ref, s_ref, y_ref, so_ref, *, gammas, kscale):
    bb, heads, dk = q_ref.shape
    cos = cos_ref[...]
    sin = sin_ref[...]
    pad = jnp.zeros((dk - 2 * heads, dk), F32)
    for bi in range(bb):
        q = _rotate(q_ref[bi], cos, sin)
        k = _rotate(k_ref[bi], cos, sin) * kscale
        qk_t = jnp.concatenate([q, k, pad], axis=0).T
        for h in range(heads):
            q_col = qk_t[:, h:h + 1]
            k_col = qk_t[:, heads + h:heads + h + 1]
            state = s_ref[bi, h] * gammas[h] + k_col * v_ref[bi, h:h + 1, :]
            so_ref[bi, h] = state
            out = jnp.sum(q_col * state, axis=0, keepdims=True)
            ms = jnp.mean(out * out, axis=-1, keepdims=True)
            y_ref[bi, h:h + 1, :] = out * lax.rsqrt(ms + NORM_EPS) * _silu(g_ref[bi, h:h + 1, :])


def _ret_decode(q, k, v, gate, state, pos, bb):
    bsz, heads, dk = q.shape
    dv = v.shape[-1]
    gammas = tuple(1.0 - 2.0 ** (-5.0 - h) for h in range(heads))
    cos, sin = _rope_tables(jnp.full((1,), pos), dk)
    return pl.pallas_call(
        functools.partial(_ret_decode_kernel, gammas=gammas, kscale=dk ** -0.5),
        grid=(bsz // bb,),
        in_specs=[pl.BlockSpec((bb, heads, dk), lambda i: (i, 0, 0)),
                  pl.BlockSpec((bb, heads, dk), lambda i: (i, 0, 0)),
                  pl.BlockSpec((bb, heads, dv), lambda i: (i, 0, 0)),
                  pl.BlockSpec((bb, heads, dv), lambda i: (i, 0, 0)),
                  pl.BlockSpec((1, dk), lambda i: (0, 0)),
                  pl.BlockSpec((1, dk), lambda i: (0, 0)),
                  pl.BlockSpec((bb, heads, dk, dv), lambda i: (i, 0, 0, 0))],
        out_specs=[pl.BlockSpec((bb, heads, dv), lambda i: (i, 0, 0)),
                   pl.BlockSpec((bb, heads, dk, dv), lambda i: (i, 0, 0, 0))],
        out_shape=[jax.ShapeDtypeStruct((bsz, heads, dv), F32),
                   jax.ShapeDtypeStruct((bsz, heads, dk, dv), F32)],
        compiler_params=_params("parallel"),
        name="ret_decode",
    )(q, k, v, gate, cos, sin, state)


def _dt_decode_kernel(dt_ref, bias_ref, alog_ref, dt_out_ref, dec_out_ref):
    dt = _softplus(dt_ref[...] + bias_ref[...])
    dt_out_ref[...] = dt
    dec_out_ref[...] = jnp.exp(dt * (-jnp.exp(alog_ref[...])))


def _dt_decode(dt_raw, dt_bias, a_log):
    bsz, heads = dt_raw.shape
    return pl.pallas_call(
        _dt_decode_kernel,
        out_shape=[jax.ShapeDtypeStruct((bsz, heads), F32), jax.ShapeDtypeStruct((bsz, heads), F32)],
        name="dt_decode",
    )(dt_raw, dt_bias.reshape(1, heads), a_log.reshape(1, heads))


def _ssd_decode_kernel(dec_ref, x_ref, dt_ref, z_ref, b_ref, c_ref, dsk_ref, nw_ref, h_ref, y_ref, ho_ref,
                       *, groups, hpg, hdim):
    bb = x_ref.shape[0]
    gw = hpg * hdim
    dstate = b_ref.shape[-1] // groups
    first = lax.broadcasted_iota(jnp.int32, (SUBLANES, gw), 0) == 0
    first_s = lax.broadcasted_iota(jnp.int32, (SUBLANES, dstate), 0) == 0
    b0 = pl.program_id(0) * bb
    for bi in range(bb):
        for g in range(groups):
            cols = slice(g * gw, (g + 1) * gw)
            scols = slice(g * dstate, (g + 1) * dstate)
            x = x_ref[bi, :, cols]
            dtx = x * dt_ref[bi, :, cols]
            a8 = jnp.where(first, jnp.broadcast_to(dtx, (SUBLANES, gw)), 0.0).astype(BF16)
            b8 = jnp.where(first_s, jnp.broadcast_to(b_ref[bi, :, scols], (SUBLANES, dstate)), 0.0).astype(BF16)
            c8 = jnp.where(first_s, jnp.broadcast_to(c_ref[bi, :, scols], (SUBLANES, dstate)), 0.0).astype(BF16)
            contrib = _dot(a8, b8, TN_DIMS)
            new = []
            for e in range(hpg):
                rows = slice(g * gw + e * hdim, g * gw + (e + 1) * hdim)
                h_new = h_ref[bi, rows, :] * dec_ref[b0 + bi, g * hpg + e] + contrib[e * hdim:(e + 1) * hdim]
                ho_ref[bi, rows, :] = h_new
                new.append(h_new)
            h_g = jnp.concatenate(new, axis=0).astype(BF16)
            y = _dot(c8, h_g, NT_DIMS)[0:1] + dsk_ref[:, cols] * x
            y = y * _silu(z_ref[bi, :, cols])
            ms = jnp.mean(y * y, axis=-1, keepdims=True)
            y_ref[bi, :, cols] = y * lax.rsqrt(ms + NORM_EPS) * nw_ref[:, cols]


def _ssd_decode(xs, bm, cm, z, dt, dec, d_skip, norm_w, state, groups, hpg, hdim, bb):
    bsz, d_inner = xs.shape
    heads = groups * hpg
    dstate = state.shape[-1]
    gs = groups * dstate
    r3 = lambda a: a.reshape(bsz, 1, a.shape[-1])
    dt_rep = jnp.repeat(dt, hdim, axis=1)
    row3 = lambda w: pl.BlockSpec((bb, 1, w), lambda i: (i, 0, 0))
    y, h = pl.pallas_call(
        functools.partial(_ssd_decode_kernel, groups=groups, hpg=hpg, hdim=hdim),
        grid=(bsz // bb,),
        in_specs=[pl.BlockSpec(memory_space=pltpu.SMEM),
                  row3(d_inner), row3(d_inner), row3(d_inner), row3(gs), row3(gs),
                  pl.BlockSpec((1, d_inner), lambda i: (0, 0)),
                  pl.BlockSpec((1, d_inner), lambda i: (0, 0)),
                  pl.BlockSpec((bb, heads * hdim, dstate), lambda i: (i, 0, 0))],
        out_specs=[row3(d_inner),
                   pl.BlockSpec((bb, heads * hdim, dstate), lambda i: (i, 0, 0))],
        out_shape=[jax.ShapeDtypeStruct((bsz, 1, d_inner), F32),
                   jax.ShapeDtypeStruct((bsz, heads * hdim, dstate), F32)],
        compiler_params=_params("parallel"),
        name="ssd_decode",
    )(dec, r3(xs), r3(dt_rep), r3(z), r3(bm), r3(cm),
      jnp.repeat(d_skip, hdim).reshape(1, d_inner), norm_w.reshape(1, d_inner),
      state.reshape(bsz, heads * hdim, dstate))
    return y.reshape(bsz, d_inner), h.reshape(bsz, heads, hdim, dstate)


def _conv_decode_kernel(*refs, width, nbranch):
    x_refs = refs[0:nbranch]
    st_refs = refs[nbranch:2 * nbranch]
    w_refs = refs[2 * nbranch:3 * nbranch]
    b_refs = refs[3 * nbranch:4 * nbranch]
    o_ref = refs[4 * nbranch]
    conv = []
    for x_ref, st_ref, w_ref, b_ref in zip(x_refs, st_refs, w_refs, b_refs):
        out = b_ref[...] + x_ref[...] * w_ref[width - 1:width, :]
        for t in range(width - 1):
            out = out + st_ref[t] * w_ref[t:t + 1, :]
        conv.append(out)
    out = _silu(conv[0]) * conv[1] if nbranch == 2 else _silu(conv[0])
    o_ref[...] = out.astype(o_ref.dtype)


def _conv_decode(x, state_t, w, b, gated, out_dtype, tc):
    bsz, c = x.shape
    width = w.shape[0]
    nbranch = 2 if gated else 1
    ncols = c // nbranch
    nj = ncols // tc
    offs = [br * nj for br in range(nbranch)]
    b2 = b.reshape(1, c)

    def specs(shape, lead):
        return [pl.BlockSpec(shape, lambda j, o=o: lead + (o + j,)) for o in offs]

    return pl.pallas_call(
        functools.partial(_conv_decode_kernel, width=width, nbranch=nbranch),
        grid=(nj,),
        in_specs=specs((bsz, tc), (0,)) + specs((width - 1, bsz, tc), (0, 0))
                 + specs((width, tc), (0,)) + specs((1, tc), (0,)),
        out_specs=pl.BlockSpec((bsz, tc), lambda j: (0, j)),
        out_shape=jax.ShapeDtypeStruct((bsz, ncols), out_dtype),
        compiler_params=_params("parallel"),
        name="conv_decode",
    )(*([x] * nbranch), *([state_t] * nbranch), *([w] * nbranch), *([b2] * nbranch))


def _ffn(x_f32, x_bf16, conv_state, w_up, conv_w, conv_b, w_down, ln_g, ln_b, alpha, length):
    rows = x_f32.shape[0]
    f = w_up.shape[1] // 2
    width = conv_w.shape[0]
    if conv_state is None:
        tm = min(CONV_ROWS, length)
        act, tails = _matmul_conv(x_bf16, w_up, (0, f), conv_w, conv_b, (0, f), f, length, BF16, tm, 512,
                                  CONV_SUB_ROWS)
        new_state = jnp.concatenate([_tile_tails(t, rows // length, length, tm, width - 1) for t in tails],
                                    axis=-1)
    else:
        hid = _matmul(x_bf16, w_up, F32, 128, 1024)
        act = _conv_decode(hid, jnp.swapaxes(conv_state, 0, 1), conv_w, conv_b, True, BF16, 512)
        new_state = jnp.concatenate([conv_state[:, 1:], hid[:, None]], axis=1)
    xf, xb = _matmul_res_ln([act], w_down, x_f32, ln_g, ln_b, alpha, RES_LN_ROWS)
    return xf, xb, new_state


def kernel(x_prompt, x_sample, cache_k, cache_v, page_table, state_ret, state_ssm, state_ssm_conv,
           state_ffn_conv, ln1_g, ln1_b, ln2_g, ln2_b, w_in_ab, w_out_ab, sb_bias, w_in_ssd, conv_w_ssd,
           conv_b_ssd, dt_bias, a_log, d_skip, norm_w_ssd, w_out_ssd, w_up, conv_w_ffn, conv_b_ffn, w_down):
    bp, seq, d_model = x_prompt.shape
    bs = x_sample.shape[0]
    depth = ln1_g.shape[0]
    alpha = (2 * depth) ** 0.25
    past_len = page_table.shape[1] * cache_k.shape[2]
    sb_heads, sb_dim = cache_k.shape[3], cache_k.shape[4]
    sb_w = sb_heads * sb_dim
    ret_heads, ret_dk, ret_dv = state_ret.shape[2], state_ret.shape[3], state_ret.shape[4]
    ret_qk_w, ret_v_w = ret_heads * ret_dk, ret_heads * ret_dv
    ssm_heads, ssm_hdim, ssm_dstate = state_ssm.shape[2], state_ssm.shape[3], state_ssm.shape[4]
    d_inner = ssm_heads * ssm_hdim
    conv_dim = state_ssm_conv.shape[-1]
    groups = (conv_dim - d_inner) // (2 * ssm_dstate)
    hpg = ssm_heads // groups
    ssm_width = conv_w_ssd.shape[1]

    xp_f = x_prompt.reshape(bp * seq, d_model)
    xs_f = x_sample.reshape(bs, d_model)
    xp_b = xp_f.astype(BF16)
    xs_b = xs_f.astype(BF16)

    outs = dict(k_p=[], v_p=[], k_s=[], v_s=[], ret_p=[], ret_s=[], ssm_p=[], ssm_s=[],
                sconv_p=[], sconv_s=[], fconv_p=[], fconv_s=[])
    for layer in range(depth):
        idx = layer // 2
        if layer % 2 == 0:
            w_in = w_in_ab[idx].astype(BF16)
            w_out = w_out_ab[idx].astype(BF16)
            q_off, k_off, v_off = 0, sb_w, 2 * sb_w
            rq_off = 3 * sb_w
            rk_off = rq_off + ret_qk_w
            rv_off = rk_off + ret_qk_w
            rg_off = rv_off + ret_v_w
            ret_w = 2 * ret_qk_w + 2 * ret_v_w
            rk_c, rv_c, rg_c = ret_qk_w, 2 * ret_qk_w, 2 * ret_qk_w + ret_v_w
            q = _matmul(xp_b, w_in, BF16, 1024, 1024, q_off, sb_w)
            k = _matmul(xp_b, w_in, F32, 1024, 1024, k_off, sb_w)
            v = _matmul(xp_b, w_in, F32, 1024, 1024, v_off, sb_w)
            ret = _matmul(xp_b, w_in, F32, 1024, 1024, rq_off, ret_w)
            ya = _sb_prompt(q, k, v, sb_bias[idx], bp, seq, sb_heads)
            yr, s_new = _ret_prompt(ret, bp, seq, ret_heads, ret_dk, ret_dv, 0, rk_c, rv_c, rg_c)
            outs['k_p'].append(k.reshape(bp, seq, sb_heads, sb_dim))
            outs['v_p'].append(v.reshape(bp, seq, sb_heads, sb_dim))
            outs['ret_p'].append(s_new)
            xp_f, xp_b = _matmul_res_ln([ya, yr], w_out, xp_f, ln1_g[layer], ln1_b[layer], alpha, RES_LN_ROWS)
            q = _matmul(xs_b, w_in, F32, 128, 1024, q_off, sb_w)
            k = _matmul(xs_b, w_in, F32, 128, 1024, k_off, sb_w)
            v = _matmul(xs_b, w_in, F32, 128, 1024, v_off, sb_w)
            ret = _matmul(xs_b, w_in, F32, 128, 1024, rq_off, ret_w)
            ya = _sb_decode(q.reshape(bs, sb_heads, sb_dim), sb_bias[idx], cache_k[idx], cache_v[idx],
                            page_table, 8)
            yr, s_new = _ret_decode(ret[:, :ret_qk_w].reshape(bs, ret_heads, ret_dk),
                                    ret[:, ret_qk_w:2 * ret_qk_w].reshape(bs, ret_heads, ret_dk),
                                    ret[:, 2 * ret_qk_w:2 * ret_qk_w + ret_v_w].reshape(bs, ret_heads, ret_dv),
                                    ret[:, 2 * ret_qk_w + ret_v_w:].reshape(bs, ret_heads, ret_dv),
                                    state_ret[idx], past_len, 4)
            outs['k_s'].append(k.reshape(bs, 1, sb_heads, sb_dim))
            outs['v_s'].append(v.reshape(bs, 1, sb_heads, sb_dim))
            outs['ret_s'].append(s_new)
            xs_f, xs_b = _matmul_res_ln([ya.reshape(bs, sb_w).astype(BF16), yr.reshape(bs, ret_v_w).astype(BF16)],
                                        w_out, xs_f, ln1_g[layer], ln1_b[layer], alpha, RES_LN_ROWS)
        else:
            main_w = d_inner + conv_dim
            w_in = w_in_ssd[idx].astype(BF16)
            w_dt = w_in[:, main_w:]
            w_out = w_out_ssd[idx].astype(BF16)
            gs = groups * ssm_dstate
            tm = min(CONV_ROWS, seq)
            z = _matmul(xp_b, w_in, F32, 1024, 1024, 0, d_inner)
            dt_raw = _matmul(xp_b, w_dt, F32, 1024, ssm_heads)
            xbc, tails = _matmul_conv(xp_b, w_in, (d_inner,), conv_w_ssd[idx], conv_b_ssd[idx], (0,), conv_dim,
                                      seq, F32, tm, 512, CONV_SUB_ROWS)
            y, h_new = _ssd_prompt(xbc, z, dt_raw, dt_bias[idx], a_log[idx], d_skip[idx], norm_w_ssd[idx],
                                   bp, seq, groups, hpg, ssm_hdim, ssm_dstate)
            outs['ssm_p'].append(h_new)
            outs['sconv_p'].append(_tile_tails(tails[0], bp, seq, tm, ssm_width - 1))
            xp_f, xp_b = _matmul_res_ln([y], w_out, xp_f, ln1_g[layer], ln1_b[layer], alpha, RES_LN_ROWS)
            z = _matmul(xs_b, w_in, F32, 128, 1024, 0, d_inner)
            dt_raw = _matmul(xs_b, w_dt, F32, 128, ssm_heads)
            xbc_raw = _matmul(xs_b, w_in, F32, 128, 1024, d_inner, conv_dim)
            xbc = _conv_decode(xbc_raw, jnp.swapaxes(state_ssm_conv[idx], 0, 1), conv_w_ssd[idx],
                               conv_b_ssd[idx], False, F32, 1024)
            dt, dec = _dt_decode(dt_raw, dt_bias[idx], a_log[idx])
            y, h_new = _ssd_decode(xbc[:, :d_inner], xbc[:, d_inner:d_inner + gs], xbc[:, d_inner + gs:],
                                   z, dt, dec, d_skip[idx], norm_w_ssd[idx], state_ssm[idx],
                                   groups, hpg, ssm_hdim, 2)
            outs['ssm_s'].append(h_new)
            outs['sconv_s'].append(jnp.concatenate([state_ssm_conv[idx][:, 1:], xbc_raw[:, None]], axis=1))
            xs_f, xs_b = _matmul_res_ln([y.astype(BF16)], w_out, xs_f, ln1_g[layer], ln1_b[layer], alpha,
                                        RES_LN_ROWS)

        wu = w_up[layer].astype(BF16)
        wd = w_down[layer].astype(BF16)
        xp_f, xp_b, fc = _ffn(xp_f, xp_b, None, wu, conv_w_ffn[layer], conv_b_ffn[layer], wd,
                              ln2_g[layer], ln2_b[layer], alpha, seq)
        outs['fconv_p'].append(fc)
        xs_f, xs_b, fc = _ffn(xs_f, xs_b, state_ffn_conv[layer], wu, conv_w_ffn[layer], conv_b_ffn[layer], wd,
                              ln2_g[layer], ln2_b[layer], alpha, 1)
        outs['fconv_s'].append(fc)

    st = lambda key: outs[key][0][None] if len(outs[key]) == 1 else jnp.stack(outs[key])
    return (xp_f.reshape(bp, seq, d_model), xs_f.reshape(bs, 1, d_model),
            st('k_p'), st('v_p'), st('k_s'), st('v_s'), st('ret_p'), st('ret_s'),
            st('ssm_p'), st('ssm_s'), st('sconv_p'), st('sconv_s'), st('fconv_p'), st('fconv_s'))
```

```python
import functools
import math

import jax
import jax.numpy as jnp
from jax import lax
from jax.experimental import pallas as pl
from jax.experimental.pallas import tpu as pltpu

F32 = jnp.float32
BF16 = jnp.bfloat16

LANES = 128
SUBLANES = 8
VMEM_LIMIT_BYTES = 52 * 1024 * 1024

CHUNK = 128
LN_EPS = 1e-5
NORM_EPS = 1e-5
ROPE_BASE = 10000.0

NT_DIMS = (((1,), (1,)), ((), ()))
TN_DIMS = (((0,), (0,)), ((), ()))


def _params(*semantics):
    return pltpu.CompilerParams(dimension_semantics=semantics, vmem_limit_bytes=VMEM_LIMIT_BYTES)


def _softplus(z):
    return jnp.maximum(z, 0.0) + jnp.log(1.0 + jnp.exp(-jnp.abs(z)))


LOG2E = math.log2(math.e)


def _softplus_bits(z2):
    return jnp.maximum(z2, 0.0) + jnp.log2(1.0 + jnp.exp2(-jnp.abs(z2)))


def _silu(x):
    return x * jax.nn.sigmoid(x)


def _dot(a, b, dims=None):
    if dims is None:
        return jnp.dot(a, b, preferred_element_type=F32)
    return lax.dot_general(a, b, dims, preferred_element_type=F32)


def _dot_split(a, m, dims=None, terms=3):
    out = None
    rem = a
    for t in range(terms):
        part = rem.astype(BF16)
        if t + 1 < terms:
            rem = rem - part.astype(F32)
        out_t = _dot(part, m, dims)
        out = out_t if out is None else out + out_t
    return out


def _mm_kernel(x_ref, w_ref, o_ref):
    o_ref[...] = _dot(x_ref[...], w_ref[...]).astype(o_ref.dtype)


def _matmul(x, w, layer, out_dtype, tm, tn, col0=0, ncols=None):
    m, k = x.shape
    n = w.shape[2] - col0 if ncols is None else ncols
    tm, tn = min(tm, m), min(tn, n)
    assert m % tm == 0 and n % tn == 0 and col0 % tn == 0
    c0 = col0 // tn
    return pl.pallas_call(
        _mm_kernel,
        grid=(n // tn, m // tm),
        in_specs=[pl.BlockSpec((tm, k), lambda j, i: (i, 0)),
                  pl.BlockSpec((None, k, tn), lambda j, i: (layer, 0, c0 + j))],
        out_specs=pl.BlockSpec((tm, tn), lambda j, i: (i, j)),
        out_shape=jax.ShapeDtypeStruct((m, n), out_dtype),
        compiler_params=_params("parallel", "parallel"),
        name="matmul",
    )(x, w)


def _mm_small_kernel(x_ref, w_ref, o_ref):
    kk = pl.program_id(0)
    tk = w_ref.shape[0]
    start = pl.multiple_of(kk * tk, tk)
    part = _dot(x_ref[:, pl.ds(start, tk)], w_ref[...])

    @pl.when(kk == 0)
    def _():
        o_ref[...] = part

    @pl.when(kk > 0)
    def _():
        o_ref[...] += part


def _matmul_small(x, w, layer, tk, ncols=None):
    m, k = x.shape
    n = w.shape[2] if ncols is None else ncols
    assert k % tk == 0
    return pl.pallas_call(
        _mm_small_kernel,
        grid=(k // tk,),
        in_specs=[pl.BlockSpec((m, k), lambda kk: (0, 0)),
                  pl.BlockSpec((None, tk, n), lambda kk: (layer, kk, 0))],
        out_specs=pl.BlockSpec((m, n), lambda kk: (0, 0)),
        out_shape=jax.ShapeDtypeStruct((m, n), F32),
        compiler_params=_params("arbitrary"),
        name="matmul_small",
    )(x, w)


def _mm_res_ln_kernel(*refs, widths, alpha):
    nseg = len(widths)
    a_refs = refs[:nseg]
    w_ref, r_ref, g_ref, b_ref, of_ref, ob_ref = refs[nseg:]
    y = alpha * r_ref[...]
    lo = 0
    for a_ref, width in zip(a_refs, widths):
        y = y + _dot(a_ref[...], w_ref[lo:lo + width, :])
        lo += width
    mu = jnp.mean(y, axis=-1, keepdims=True)
    d = y - mu
    var = jnp.mean(d * d, axis=-1, keepdims=True)
    out = d * lax.rsqrt(var + LN_EPS) * g_ref[...] + b_ref[...]
    of_ref[...] = out
    ob_ref[...] = out.astype(BF16)


def _matmul_res_ln(acts, w, layer, resid, gain, bias, alpha, tm):
    m = acts[0].shape[0]
    _, k, d = w.shape
    tm = min(tm, m)
    widths = tuple(a.shape[1] for a in acts)
    assert m % tm == 0 and sum(widths) == k
    return pl.pallas_call(
        functools.partial(_mm_res_ln_kernel, widths=widths, alpha=alpha),
        grid=(m // tm,),
        in_specs=[pl.BlockSpec((tm, width), lambda i: (i, 0)) for width in widths] + [
                  pl.BlockSpec((None, k, d), lambda i: (layer, 0, 0), pipeline_mode=pl.Buffered(1)),
                  pl.BlockSpec((tm, d), lambda i: (i, 0)),
                  pl.BlockSpec((1, d), lambda i: (0, 0)),
                  pl.BlockSpec((1, d), lambda i: (0, 0))],
        out_specs=[pl.BlockSpec((tm, d), lambda i: (i, 0)),
                   pl.BlockSpec((tm, d), lambda i: (i, 0))],
        out_shape=[jax.ShapeDtypeStruct((m, d), F32), jax.ShapeDtypeStruct((m, d), BF16)],
        compiler_params=_params("parallel"),
        name="matmul_res_ln",
    )(*acts, w, resid, gain.reshape(1, d), bias.reshape(1, d))


def _sb_prompt_kernel(bias_ref, q_ref, k_ref, v_ref, tri_ref, o_ref, *, scale, tile):
    h = pl.program_id(1)
    qi = pl.program_id(2)
    bias2 = bias_ref[h] * LOG2E
    q = q_ref[...].astype(BF16)
    tri = tri_ref[...]
    row = lax.broadcasted_iota(jnp.int32, (tile, tile), 0)
    col = lax.broadcasted_iota(jnp.int32, (tile, tile), 1)
    valid = col < row
    nsub = tile // CHUNK

    def group(g, carry, acc, diagonal):
        start = pl.multiple_of(g * tile, tile)
        k = k_ref[pl.ds(start, tile), :].astype(BF16)
        v = v_ref[pl.ds(start, tile), :].astype(BF16)
        z2 = _dot(q, k, NT_DIMS) * (scale * LOG2E) + bias2
        sp = _softplus_bits(z2)
        fail = jnp.where(valid, sp, 0.0) if diagonal else sp
        hi = fail.astype(BF16)
        lo = (fail - hi.astype(F32)).astype(BF16)
        later = [None] * nsub
        for c in range(nsub - 1, -1, -1):
            cols = slice(c * CHUNK, (c + 1) * CHUNK)
            sums = _dot(jnp.concatenate([hi[:, cols], lo[:, cols]], axis=1), tri)
            later[c] = sums[:, :CHUNK] + carry
            carry = carry + sums[:, CHUNK:]
        w = jnp.exp2(z2 - sp + jnp.concatenate(later, axis=1))
        if diagonal:
            w = jnp.where(valid, w, 0.0)
        acc = acc + _dot(w.astype(BF16), v)
        return carry, acc

    zeros = jnp.zeros((tile, CHUNK), F32)
    carry, acc = group(qi, zeros, zeros, True)

    def body(i, c):
        return group(qi - 1 - i, c[0], c[1], False)

    carry, acc = lax.fori_loop(0, qi, body, (carry, acc))
    o_ref[...] = acc.astype(o_ref.dtype)


def _sb_prompt(q, k, v, bias, bsz, length, heads, tile=512):
    tile = min(tile, length)
    nq = length // tile
    j = lax.broadcasted_iota(jnp.int32, (CHUNK, CHUNK), 0)
    s = lax.broadcasted_iota(jnp.int32, (CHUNK, CHUNK), 1)
    tri = -jnp.concatenate([(j > s).astype(BF16), jnp.ones((CHUNK, CHUNK), BF16)], axis=1)
    tri = jnp.concatenate([tri, tri], axis=0)
    return pl.pallas_call(
        functools.partial(_sb_prompt_kernel, scale=CHUNK ** -0.5, tile=tile),
        grid=(bsz, heads, nq),
        in_specs=[pl.BlockSpec(memory_space=pltpu.SMEM),
                  pl.BlockSpec((tile, CHUNK), lambda b, h, i: (b * nq + i, h)),
                  pl.BlockSpec((length, CHUNK), lambda b, h, i: (b, h)),
                  pl.BlockSpec((length, CHUNK), lambda b, h, i: (b, h)),
                  pl.BlockSpec((2 * CHUNK, 2 * CHUNK), lambda b, h, i: (0, 0))],
        out_specs=pl.BlockSpec((tile, CHUNK), lambda b, h, i: (b * nq + i, h)),
        out_shape=jax.ShapeDtypeStruct((bsz * length, heads * CHUNK), BF16),
        compiler_params=_params("parallel", "parallel", "arbitrary"),
        name="sb_prompt",
    )(bias, q, k, v, tri)


def _rotate(x, cos, sin_signed):
    return x * cos + pltpu.roll(x, x.shape[-1] // 2, x.ndim - 1) * sin_signed


def _ret_prompt_kernel(q_ref, k_ref, v_ref, g_ref, cos_ref, sin_ref, dec_ref, te_ref, fs_ref, cd_ref,
                       y_ref, so_ref, s_ref, *, nchunks, kscale, dk, dv):
    c = pl.program_id(2)

    @pl.when(c == 0)
    def _():
        s_ref[...] = jnp.zeros_like(s_ref)

    cos = cos_ref[...]
    sin = sin_ref[...]
    for h in range(s_ref.shape[0]):
        qk = slice(h * dk, (h + 1) * dk)
        vg = slice(h * dv, (h + 1) * dv)
        q = _rotate(q_ref[:, qk], cos, sin)
        k = _rotate(k_ref[:, qk], cos, sin) * kscale
        v = v_ref[:, vg].astype(BF16)
        scores = _dot(q.astype(BF16), k.astype(BF16), NT_DIMS) * dec_ref[h]
        out = _dot(scores.astype(BF16), v)
        state = s_ref[h]
        out = out + _dot((q * fs_ref[h]).astype(BF16), state.astype(BF16))
        state = state * cd_ref[h] + _dot((k * te_ref[h]).astype(BF16), v, TN_DIMS)
        s_ref[h] = state
        ms = jnp.mean(out * out, axis=-1, keepdims=True)
        y = out * lax.rsqrt(ms + NORM_EPS) * _silu(g_ref[:, vg])
        y_ref[:, vg] = y.astype(y_ref.dtype)

    @pl.when(c == nchunks - 1)
    def _():
        so_ref[0] = s_ref[...]


def _ret_consts(heads, dk):
    lg = jnp.log(1.0 - 2.0 ** (-5.0 - jnp.arange(heads, dtype=F32)))
    idx = jnp.arange(CHUNK, dtype=F32)
    gap = idx[:, None] - idx[None, :]
    dec = jnp.where(gap >= 0, jnp.exp(lg[:, None, None] * jnp.maximum(gap, 0.0)), 0.0)
    te = jnp.exp(lg[:, None] * (CHUNK - 1.0 - idx)[None, :])
    fs = jnp.exp(lg[:, None] * (idx + 1.0)[None, :])
    rep = lambda a: jnp.broadcast_to(a[:, :, None], (heads, CHUNK, dk))
    return lg, dec, rep(te), rep(fs)


def _rope_tables(pos, dk):
    half = dk // 2
    inv_freq = ROPE_BASE ** (-jnp.linspace(0.0, 1.0, half, dtype=F32))
    ang = pos.astype(F32)[:, None] * inv_freq[None, :]
    cos, sin = jnp.cos(ang), jnp.sin(ang)
    return jnp.concatenate([cos, cos], axis=-1), jnp.concatenate([-sin, sin], axis=-1)


def _ret_prompt(proj, bsz, length, heads, dk, dv, q_col, k_col, v_col, g_col, hb=4):
    nc = length // CHUNK
    assert heads % hb == 0 and all(c % (hb * dk) == 0 for c in (q_col, k_col))
    assert all(c % (hb * dv) == 0 for c in (v_col, g_col))
    lg, dec, te, fs = _ret_consts(heads, dk)
    cd = jnp.broadcast_to(jnp.exp(lg * CHUNK)[:, None, None], (heads, 1, dv))
    cos, sin = _rope_tables(jnp.arange(length), dk)
    row = lambda b, c: b * nc + c
    qk_spec = lambda col: pl.BlockSpec((CHUNK, hb * dk), lambda b, h, c: (row(b, c), col // (hb * dk) + h))
    vg_spec = lambda col: pl.BlockSpec((CHUNK, hb * dv), lambda b, h, c: (row(b, c), col // (hb * dv) + h))
    head_spec = lambda r, w: pl.BlockSpec((hb, r, w), lambda b, h, c: (h, 0, 0))
    return pl.pallas_call(
        functools.partial(_ret_prompt_kernel, nchunks=nc, kscale=dk ** -0.5, dk=dk, dv=dv),
        grid=(bsz, heads // hb, nc),
        in_specs=[qk_spec(q_col), qk_spec(k_col), vg_spec(v_col), vg_spec(g_col),
                  pl.BlockSpec((CHUNK, dk), lambda b, h, c: (c, 0)),
                  pl.BlockSpec((CHUNK, dk), lambda b, h, c: (c, 0)),
                  head_spec(CHUNK, CHUNK), head_spec(CHUNK, dk), head_spec(CHUNK, dk), head_spec(1, dv)],
        out_specs=[pl.BlockSpec((CHUNK, hb * dv), lambda b, h, c: (row(b, c), h)),
                   pl.BlockSpec((1, hb, dk, dv), lambda b, h, c: (b, h, 0, 0))],
        out_shape=[jax.ShapeDtypeStruct((bsz * length, heads * dv), BF16),
                   jax.ShapeDtypeStruct((bsz, heads, dk, dv), F32)],
        scratch_shapes=[pltpu.VMEM((hb, dk, dv), F32)],
        compiler_params=_params("parallel", "parallel", "arbitrary"),
        name="ret_prompt",
    )(proj, proj, proj, proj, cos, sin, dec, te, fs, cd)


HALO_ROWS = 16
RES_LN_ROWS = 256
CONV_ROWS = 2048
CONV_SUB_ROWS = 1024


def _mm_conv_kernel(x_ref, xh_ref, *refs, width, tiles_per_seq, nbranch, sub):
    w_refs = refs[0:nbranch]
    cw_refs = refs[nbranch:2 * nbranch]
    cb_refs = refs[2 * nbranch:3 * nbranch]
    o_ref = refs[3 * nbranch]
    tail_refs = refs[3 * nbranch + 1:4 * nbranch + 1]
    ext_refs = refs[4 * nbranch + 1:]
    tm = x_ref.shape[0]
    at_start = (pl.program_id(1) % tiles_per_seq) == 0
    xh = xh_ref[...]
    for br in range(nbranch):
        before = _dot(xh, w_refs[br][...])[HALO_ROWS - SUBLANES:]
        ext_refs[br][0:SUBLANES, :] = jnp.where(at_start, 0.0, before)
    for r in range(tm // sub):
        rows = slice(r * sub, (r + 1) * sub)
        base = SUBLANES + r * sub
        conv = []
        for br in range(nbranch):
            h = _dot(x_ref[rows, :], w_refs[br][...])
            ext_refs[br][base:base + sub, :] = h
            acc = cb_refs[br][...] + h * cw_refs[br][width - 1:width, :]
            for t in range(width - 1):
                back = width - 1 - t
                acc = acc + ext_refs[br][base - back:base - back + sub, :] * cw_refs[br][t:t + 1, :]
            conv.append(acc)
        out = _silu(conv[0]) * conv[1] if nbranch == 2 else _silu(conv[0])
        o_ref[rows, :] = out.astype(o_ref.dtype)
    for br in range(nbranch):
        tail_refs[br][...] = ext_refs[br][tm:tm + SUBLANES, :]


def _matmul_conv(x, w, layer, w_cols, conv_w, conv_b, conv_cols, ncols, length, out_dtype, tm, tn, sub):
    m, k = x.shape
    width = conv_w.shape[0]
    nbranch = len(w_cols)
    assert m % tm == 0 and length % tm == 0 and ncols % tn == 0 and tm % HALO_ROWS == 0
    assert all(c % tn == 0 for c in list(w_cols) + list(conv_cols)) and width - 1 <= SUBLANES
    nj = ncols // tn
    halo_step = tm // HALO_ROWS
    cb2 = conv_b.reshape(1, -1)

    def col_spec(rows, c0):
        return pl.BlockSpec((rows, tn), lambda j, i: (0, c0 // tn + j))

    outs = pl.pallas_call(
        functools.partial(_mm_conv_kernel, width=width, tiles_per_seq=length // tm, nbranch=nbranch,
                          sub=min(sub, tm)),
        grid=(nj, m // tm),
        in_specs=[pl.BlockSpec((tm, k), lambda j, i: (i, 0)),
                  pl.BlockSpec((HALO_ROWS, k), lambda j, i: (jnp.maximum(i * halo_step - 1, 0), 0))]
                 + [pl.BlockSpec((None, k, tn), lambda j, i, c=c: (layer, 0, c // tn + j)) for c in w_cols]
                 + [col_spec(width, c) for c in conv_cols]
                 + [col_spec(1, c) for c in conv_cols],
        out_specs=[pl.BlockSpec((tm, tn), lambda j, i: (i, j))]
                  + [pl.BlockSpec((SUBLANES, tn), lambda j, i: (i, j))] * nbranch,
        out_shape=[jax.ShapeDtypeStruct((m, ncols), out_dtype)]
                  + [jax.ShapeDtypeStruct((m // tm * SUBLANES, ncols), F32)] * nbranch,
        scratch_shapes=[pltpu.VMEM((tm + SUBLANES, tn), F32)] * nbranch,
        compiler_params=_params("parallel", "arbitrary"),
        name="matmul_conv",
    )(x, x, *([w] * nbranch), *([conv_w] * nbranch), *([cb2] * nbranch))
    return outs[0], outs[1:]


def _tile_tails(tails, bsz, length, tm, keep):
    per_seq = length // tm
    t = tails.reshape(bsz * per_seq, SUBLANES, tails.shape[-1])
    return t[per_seq - 1::per_seq, SUBLANES - keep:, :]


def _ssd_prompt_kernel(x_ref, b_ref, c_ref, z_ref, dtc_ref, dtr_ref, dbc_ref, dbr_ref, alc_ref, alr_ref,
                       dsk_ref, nw_ref, ltri_ref, ones_ref, y_ref, ho_ref, h_ref, *, nchunks, hpg, hdim):
    c = pl.program_id(2)

    @pl.when(c == 0)
    def _():
        h_ref[...] = jnp.zeros_like(h_ref)

    ltri = ltri_ref[...]
    ones = ones_ref[...]
    row = lax.broadcasted_iota(jnp.int32, (CHUNK, CHUNK), 0)
    col = lax.broadcasted_iota(jnp.int32, (CHUNK, CHUNK), 1)
    causal = col <= row
    pair = LANES // hdim
    slabs = hpg // pair
    gw = hpg * hdim
    dstate = b_ref.shape[1] // dtc_ref.shape[0]
    for gi in range(dtc_ref.shape[0]):
        gcols = slice(gi * gw, (gi + 1) * gw)
        scols = slice(gi * dstate, (gi + 1) * dstate)
        dtc = _softplus(dtc_ref[gi] + dbc_ref[gi])
        dtr = _softplus(dtr_ref[gi] + dbr_ref[gi])
        dac = dtc * (-jnp.exp(alc_ref[gi]))
        dar = dtr * (-jnp.exp(alr_ref[gi]))
        cumc = _dot_split_lhs01(ltri, dac)
        cumr = _dot_split(dar, ltri, NT_DIMS)
        totr = _dot_split(dar, ones)
        to_end = jnp.exp(totr - cumr) * dtr
        chunk_decay = jnp.exp(totr)

        bmat = b_ref[:, scols].astype(BF16)
        cmat = c_ref[:, scols]
        cb = _dot(cmat.astype(BF16), bmat, NT_DIMS)
        x = x_ref[:, gcols]
        ys = []
        for p in range(slabs):
            xp = x[:, p * LANES:(p + 1) * LANES]
            xpb = xp.astype(BF16)
            hp = h_ref[gi * slabs + p]
            hpb = hp.astype(BF16)
            yp = None
            te = None
            cd = None
            for e_local in range(pair):
                e = p * pair + e_local
                cum_i = jnp.broadcast_to(cumc[:, e:e + 1], (CHUNK, CHUNK))
                seg = cum_i - cumr[e:e + 1, :]
                decay = jnp.exp(jnp.where(causal, seg, -jnp.inf))
                w = (cb * decay * dtr[e:e + 1, :]).astype(BF16)
                y_e = _dot(w, xpb) + _dot((cmat * jnp.exp(cum_i)).astype(BF16), hpb, NT_DIMS)
                te_e = jnp.broadcast_to(to_end[e:e + 1, :], (CHUNK, CHUNK))
                cd_e = jnp.broadcast_to(chunk_decay[e:e + 1, :], (CHUNK, CHUNK))
                if e_local == 0:
                    yp, te, cd = y_e, te_e, cd_e
                else:
                    in_head = (col >= e_local * hdim) & (col < (e_local + 1) * hdim)
                    in_rows = (row >= e_local * hdim) & (row < (e_local + 1) * hdim)
                    yp = jnp.where(in_head, y_e, yp)
                    te = jnp.where(in_rows, te_e, te)
                    cd = jnp.where(in_rows, cd_e, cd)
            h_ref[gi * slabs + p] = hp * cd + _dot((xp.T * te).astype(BF16), bmat)
            ys.append(yp)
        y = jnp.concatenate(ys, axis=1) + dsk_ref[:, gcols] * x
        y = y * _silu(z_ref[:, gcols])
        ms = jnp.mean(y * y, axis=-1, keepdims=True)
        y = y * lax.rsqrt(ms + NORM_EPS) * nw_ref[:, gcols]
        y_ref[:, gcols] = y.astype(y_ref.dtype)

    @pl.when(c == nchunks - 1)
    def _():
        ho_ref[0] = h_ref[...]


def _dot_split_lhs01(m, a, terms=3):
    out = None
    rem = a
    for t in range(terms):
        part = rem.astype(BF16)
        if t + 1 < terms:
            rem = rem - part.astype(F32)
        out_t = _dot(m, part)
        out = out_t if out is None else out + out_t
    return out


def _ssd_prompt(xbc, zproj, dt_raw, dt_bias, a_log, d_skip, norm_w, bsz, length, groups, hpg, hdim, dstate, gb=4):
    nc = length // CHUNK
    gw = hpg * hdim
    d_inner = groups * gw
    assert dstate == CHUNK and gw % LANES == 0 and groups % gb == 0 and d_inner % (gb * dstate) == 0
    heads = groups * hpg
    rows = bsz * length
    dt3 = dt_raw.reshape(bsz, length, groups, hpg)
    dtc = jnp.transpose(dt3, (2, 0, 1, 3)).reshape(groups, rows, hpg)
    dtr = jnp.transpose(dt3, (0, 2, 3, 1)).reshape(bsz * groups, hpg, length)
    i = lax.broadcasted_iota(jnp.int32, (CHUNK, CHUNK), 0)
    j = lax.broadcasted_iota(jnp.int32, (CHUNK, CHUNK), 1)
    ltri = (j <= i).astype(BF16)
    ones = jnp.ones((CHUNK, CHUNK), BF16)
    slabs = gw // LANES
    ng = groups // gb
    b_blk0 = d_inner // (gb * dstate)
    c_blk0 = b_blk0 + ng
    row = lambda b, c: b * nc + c
    wide = pl.BlockSpec((CHUNK, gb * gw), lambda b, g, c: (row(b, c), g))
    par = lambda r, w: pl.BlockSpec((gb, r, w), lambda b, g, c: (g, 0, 0))
    vec = pl.BlockSpec((1, gb * gw), lambda b, g, c: (0, g))
    const = pl.BlockSpec((CHUNK, CHUNK), lambda b, g, c: (0, 0))
    y, h = pl.pallas_call(
        functools.partial(_ssd_prompt_kernel, nchunks=nc, hpg=hpg, hdim=hdim),
        grid=(bsz, ng, nc),
        in_specs=[wide,
                  pl.BlockSpec((CHUNK, gb * dstate), lambda b, g, c: (row(b, c), b_blk0 + g)),
                  pl.BlockSpec((CHUNK, gb * dstate), lambda b, g, c: (row(b, c), c_blk0 + g)),
                  wide,
                  pl.BlockSpec((gb, CHUNK, hpg), lambda b, g, c: (g, row(b, c), 0)),
                  pl.BlockSpec((gb, hpg, CHUNK), lambda b, g, c: (b * ng + g, 0, c)),
                  par(1, hpg), par(hpg, 1), par(1, hpg), par(hpg, 1), vec, vec, const, const],
        out_specs=[wide,
                   pl.BlockSpec((1, gb * slabs, LANES, dstate), lambda b, g, c: (b, g, 0, 0))],
        out_shape=[jax.ShapeDtypeStruct((rows, d_inner), BF16),
                   jax.ShapeDtypeStruct((bsz, groups * slabs, LANES, dstate), F32)],
        scratch_shapes=[pltpu.VMEM((gb * slabs, LANES, dstate), F32)],
        compiler_params=_params("parallel", "parallel", "arbitrary"),
        name="ssd_prompt",
    )(xbc, xbc, xbc, zproj, dtc, dtr,
      dt_bias.reshape(groups, 1, hpg), dt_bias.reshape(groups, hpg, 1),
      a_log.reshape(groups, 1, hpg), a_log.reshape(groups, hpg, 1),
      jnp.repeat(d_skip, hdim).reshape(1, d_inner), norm_w.reshape(1, d_inner), ltri, ones)
    return y, h.reshape(bsz, heads, hdim, dstate)


def _sb_decode_kernel(pt_ref, q_ref, bias_ref, ones_ref, *refs, pages_per_step, page, scale):
    k_refs = refs[:pages_per_step]
    v_refs = refs[pages_per_step:2 * pages_per_step]
    o_ref, carry_ref, acc_ref = refs[2 * pages_per_step:]
    g = pl.program_id(1)

    @pl.when(g == 0)
    def _():
        carry_ref[...] = jnp.ones_like(carry_ref)
        acc_ref[...] = jnp.zeros_like(acc_ref)

    q2 = q_ref[0] * (scale * LOG2E)
    bias2 = bias_ref[...]
    ones = ones_ref[...]
    heads, hd = q2.shape
    seg = 16
    nseg = page // seg
    carry = carry_ref[...]
    acc = acc_ref[...]
    for i in range(pages_per_step):
        kk = k_refs[i][0]
        vv = v_refs[i][0]
        prod = (kk * q2[None]).reshape(page * heads, hd)
        z2 = _dot(prod.astype(BF16), ones).reshape(page, heads, hd) + bias2[None]
        fail = 1.0 / (1.0 + jnp.exp2(z2))
        beta = 1.0 - fail
        parts = [None] * nseg
        totals = [None] * nseg
        for s in range(nseg):
            run = None
            part = None
            for t in range((s + 1) * seg - 1, s * seg - 1, -1):
                weight = beta[t] if run is None else beta[t] * run
                term = weight * vv[t]
                part = term if part is None else part + term
                run = fail[t] if run is None else run * fail[t]
            parts[s], totals[s] = part, run
        for s in range(nseg - 1, -1, -1):
            acc = acc + carry * parts[s]
            carry = carry * totals[s]
    carry_ref[...] = carry
    acc_ref[...] = acc
    o_ref[0] = acc


def _sb_decode(q, bias, cache_k, cache_v, page_table, pages_per_step):
    bsz, heads, hd = q.shape
    page = cache_k.shape[1]
    n_pages = page_table.shape[1]
    assert n_pages % pages_per_step == 0
    steps = n_pages // pages_per_step

    def page_map(i):
        return lambda b, g, pt: (pt[b, n_pages - 1 - (g * pages_per_step + i)], 0, 0, 0)

    kv_specs = [pl.BlockSpec((1, page, heads, hd), page_map(i)) for i in range(pages_per_step)]
    grid_spec = pltpu.PrefetchScalarGridSpec(
        num_scalar_prefetch=1,
        grid=(bsz, steps),
        in_specs=[pl.BlockSpec((1, heads, hd), lambda b, g, pt: (b, 0, 0)),
                  pl.BlockSpec((heads, hd), lambda b, g, pt: (0, 0)),
                  pl.BlockSpec((hd, hd), lambda b, g, pt: (0, 0))] + kv_specs + kv_specs,
        out_specs=pl.BlockSpec((1, heads, hd), lambda b, g, pt: (b, 0, 0)),
        scratch_shapes=[pltpu.VMEM((heads, hd), F32), pltpu.VMEM((heads, hd), F32)],
    )
    bias_rep = jnp.broadcast_to((bias.astype(F32) * LOG2E)[:, None], (heads, hd))
    ones = jnp.ones((hd, hd), BF16)
    return pl.pallas_call(
        functools.partial(_sb_decode_kernel, pages_per_step=pages_per_step, page=page, scale=hd ** -0.5),
        grid_spec=grid_spec,
        out_shape=jax.ShapeDtypeStruct((bsz, heads, hd), F32),
        compiler_params=_params("parallel", "arbitrary"),
        name="sb_decode",
    )(page_table, q, bias_rep, ones, *([cache_k] * pages_per_step), *([cache_v] * pages_per_step))


def _ret_decode_kernel(q_ref, k_ref, v_ref, g_ref, cos_ref, sin_ref, s_ref, y_ref, so_ref, *, gammas, kscale):
    bb, heads, dk = q_ref.shape
    cos = cos_ref[...]
    sin = sin_ref[...]
    pad = jnp.zeros((dk - 2 * heads, dk), F32)
    for bi in range(bb):
        q = _rotate(q_ref[bi], cos, sin)
        k = _rotate(k_ref[bi], cos, sin) * kscale
        qk_t = jnp.concatenate([q, k, pad], axis=0).T
        for h in range(heads):
            q_col = qk_t[:, h:h + 1]
            k_col = qk_t[:, heads + h:heads + h + 1]
            state = s_ref[bi, h] * gammas[h] + k_col * v_ref[bi, h:h + 1, :]
            so_ref[bi, h] = state
            out = jnp.sum(q_col * state, axis=0, keepdims=True)
            ms = jnp.mean(out * out, axis=-1, keepdims=True)
            y_ref[bi, h:h + 1, :] = out * lax.rsqrt(ms + NORM_EPS) * _silu(g_ref[bi, h:h + 1, :])


def _ret_decode(q, k, v, gate, state, pos, bb):
    bsz, heads, dk = q.shape
    dv = v.shape[-1]
    gammas = tuple(1.0 - 2.0 ** (-5.0 - h) for h in range(heads))
    cos, sin = _rope_tables(jnp.full((1,), pos), dk)
    return pl.pallas_call(
        functools.partial(_ret_decode_kernel, gammas=gammas, kscale=dk ** -0.5),
        grid=(bsz // bb,),
        in_specs=[pl.BlockSpec((bb, heads, dk), lambda i: (i, 0, 0)),
                  pl.BlockSpec((bb, heads, dk), lambda i: (i, 0, 0)),
                  pl.BlockSpec((bb, heads, dv), lambda i: (i, 0, 0)),
                  pl.BlockSpec((bb, heads, dv), lambda i: (i, 0, 0)),
                  pl.BlockSpec((1, dk), lambda i: (0, 0)),
                  pl.BlockSpec((1, dk), lambda i: (0, 0)),
                  pl.BlockSpec((bb, heads, dk, dv), lambda i: (i, 0, 0, 0))],
        out_specs=[pl.BlockSpec((bb, heads, dv), lambda i: (i, 0, 0)),
                   pl.BlockSpec((bb, heads, dk, dv), lambda i: (i, 0, 0, 0))],
        out_shape=[jax.ShapeDtypeStruct((bsz, heads, dv), F32),
                   jax.ShapeDtypeStruct((bsz, heads, dk, dv), F32)],
        compiler_params=_params("parallel"),
        name="ret_decode",
    )(q, k, v, gate, cos, sin, state)


def _dt_decode_kernel(dt_ref, bias_ref, alog_ref, dt_out_ref, dec_out_ref):
    dt = _softplus(dt_ref[...] + bias_ref[...])
    dt_out_ref[...] = dt
    dec_out_ref[...] = jnp.exp(dt * (-jnp.exp(alog_ref[...])))


def _dt_decode(dt_raw, dt_bias, a_log):
    bsz, heads = dt_raw.shape
    return pl.pallas_call(
        _dt_decode_kernel,
        out_shape=[jax.ShapeDtypeStruct((bsz, heads), F32), jax.ShapeDtypeStruct((bsz, heads), F32)],
        name="dt_decode",
    )(dt_raw, dt_bias.reshape(1, heads), a_log.reshape(1, heads))


def _ssd_decode_kernel(dec_ref, x_ref, dt_ref, z_ref, b_ref, c_ref, dsk_ref, nw_ref, h_ref, y_ref, ho_ref,
                       *, groups, hpg, hdim):
    bb = x_ref.shape[0]
    gw = hpg * hdim
    dstate = b_ref.shape[-1] // groups
    first = lax.broadcasted_iota(jnp.int32, (SUBLANES, gw), 0) == 0
    first_s = lax.broadcasted_iota(jnp.int32, (SUBLANES, dstate), 0) == 0
    b0 = pl.program_id(0) * bb
    for bi in range(bb):
        for g in range(groups):
            cols = slice(g * gw, (g + 1) * gw)
            scols = slice(g * dstate, (g + 1) * dstate)
            x = x_ref[bi, :, cols]
            dtx = x * dt_ref[bi, :, cols]
            a8 = jnp.where(first, jnp.broadcast_to(dtx, (SUBLANES, gw)), 0.0).astype(BF16)
            b8 = jnp.where(first_s, jnp.broadcast_to(b_ref[bi, :, scols], (SUBLANES, dstate)), 0.0).astype(BF16)
            c8 = jnp.where(first_s, jnp.broadcast_to(c_ref[bi, :, scols], (SUBLANES, dstate)), 0.0).astype(BF16)
            contrib = _dot(a8, b8, TN_DIMS)
            new = []
            for e in range(hpg):
                rows = slice(g * gw + e * hdim, g * gw + (e + 1) * hdim)
                h_new = h_ref[bi, rows, :] * dec_ref[b0 + bi, g * hpg + e] + contrib[e * hdim:(e + 1) * hdim]
                ho_ref[bi, rows, :] = h_new
                new.append(h_new)
            h_g = jnp.concatenate(new, axis=0).astype(BF16)
            y = _dot(c8, h_g, NT_DIMS)[0:1] + dsk_ref[:, cols] * x
            y = y * _silu(z_ref[bi, :, cols])
            ms = jnp.mean(y * y, axis=-1, keepdims=True)
            y_ref[bi, :, cols] = y * lax.rsqrt(ms + NORM_EPS) * nw_ref[:, cols]


def _ssd_decode(xs, bm, cm, z, dt, dec, d_skip, norm_w, state, groups, hpg, hdim, bb):
    bsz, d_inner = xs.shape
    heads = groups * hpg
    dstate = state.shape[-1]
    gs = groups * dstate
    r3 = lambda a: a.reshape(bsz, 1, a.shape[-1])
    dt_rep = jnp.repeat(dt, hdim, axis=1)
    row3 = lambda w: pl.BlockSpec((bb, 1, w), lambda i: (i, 0, 0))
    y, h = pl.pallas_call(
        functools.partial(_ssd_decode_kernel, groups=groups, hpg=hpg, hdim=hdim),
        grid=(bsz // bb,),
        in_specs=[pl.BlockSpec(memory_space=pltpu.SMEM),
                  row3(d_inner), row3(d_inner), row3(d_inner), row3(gs), row3(gs),
                  pl.BlockSpec((1, d_inner), lambda i: (0, 0)),
                  pl.BlockSpec((1, d_inner), lambda i: (0, 0)),
                  pl.BlockSpec((bb, heads * hdim, dstate), lambda i: (i, 0, 0))],
        out_specs=[row3(d_inner),
                   pl.BlockSpec((bb, heads * hdim, dstate), lambda i: (i, 0, 0))],
        out_shape=[jax.ShapeDtypeStruct((bsz, 1, d_inner), F32),
                   jax.ShapeDtypeStruct((bsz, heads * hdim, dstate), F32)],
        compiler_params=_params("parallel"),
        name="ssd_decode",
    )(dec, r3(xs), r3(dt_rep), r3(z), r3(bm), r3(cm),
      jnp.repeat(d_skip, hdim).reshape(1, d_inner), norm_w.reshape(1, d_inner),
      state.reshape(bsz, heads * hdim, dstate))
    return y.reshape(bsz, d_inner), h.reshape(bsz, heads, hdim, dstate)


def _conv_decode_kernel(*refs, width, nbranch):
    x_refs = refs[0:nbranch]
    st_refs = refs[nbranch:2 * nbranch]
    w_refs = refs[2 * nbranch:3 * nbranch]
    b_refs = refs[3 * nbranch:4 * nbranch]
    o_ref = refs[4 * nbranch]
    conv = []
    for x_ref, st_ref, w_ref, b_ref in zip(x_refs, st_refs, w_refs, b_refs):
        out = b_ref[...] + x_ref[...] * w_ref[width - 1:width, :]
        for t in range(width - 1):
            out = out + st_ref[t] * w_ref[t:t + 1, :]
        conv.append(out)
    out = _silu(conv[0]) * conv[1] if nbranch == 2 else _silu(conv[0])
    o_ref[...] = out.astype(o_ref.dtype)


def _conv_decode(x, state_t, w, b, gated, out_dtype, tc):
    bsz, c = x.shape
    width = w.shape[0]
    nbranch = 2 if gated else 1
    ncols = c // nbranch
    nj = ncols // tc
    offs = [br * nj for br in range(nbranch)]
    b2 = b.reshape(1, c)

    def specs(shape, lead):
        return [pl.BlockSpec(shape, lambda j, o=o: lead + (o + j,)) for o in offs]

    return pl.pallas_call(
        functools.partial(_conv_decode_kernel, width=width, nbranch=nbranch),
        grid=(nj,),
        in_specs=specs((bsz, tc), (0,)) + specs((width - 1, bsz, tc), (0, 0))
                 + specs((width, tc), (0,)) + specs((1, tc), (0,)),
        out_specs=pl.BlockSpec((bsz, tc), lambda j: (0, j)),
        out_shape=jax.ShapeDtypeStruct((bsz, ncols), out_dtype),
        compiler_params=_params("parallel"),
        name="conv_decode",
    )(*([x] * nbranch), *([state_t] * nbranch), *([w] * nbranch), *([b2] * nbranch))


SMALL_TK = 256


def _ffn(x_f32, x_bf16, conv_state, w_up, w_down, layer, conv_w, conv_b, ln_g, ln_b, alpha, length):
    rows = x_f32.shape[0]
    f = w_up.shape[2] // 2
    width = conv_w.shape[0]
    if conv_state is None:
        tm = min(CONV_ROWS, length)
        act, tails = _matmul_conv(x_bf16, w_up, layer, (0, f), conv_w, conv_b, (0, f), f, length, BF16, tm, 512,
                                  CONV_SUB_ROWS)
        new_state = jnp.concatenate([_tile_tails(t, rows // length, length, tm, width - 1) for t in tails],
                                    axis=-1)
    else:
        hid = _matmul_small(x_bf16, w_up, layer, SMALL_TK)
        act = _conv_decode(hid, jnp.swapaxes(conv_state, 0, 1), conv_w, conv_b, True, BF16, 512)
        new_state = jnp.concatenate([conv_state[:, 1:], hid[:, None]], axis=1)
    xf, xb = _matmul_res_ln([act], w_down, layer, x_f32, ln_g, ln_b, alpha, RES_LN_ROWS)
    return xf, xb, new_state


def kernel(x_prompt, x_sample, cache_k, cache_v, page_table, state_ret, state_ssm, state_ssm_conv,
           state_ffn_conv, ln1_g, ln1_b, ln2_g, ln2_b, w_in_ab, w_out_ab, sb_bias, w_in_ssd, conv_w_ssd,
           conv_b_ssd, dt_bias, a_log, d_skip, norm_w_ssd, w_out_ssd, w_up, conv_w_ffn, conv_b_ffn, w_down):
    bp, seq, d_model = x_prompt.shape
    bs = x_sample.shape[0]
    depth = ln1_g.shape[0]
    alpha = (2 * depth) ** 0.25
    past_len = page_table.shape[1] * cache_k.shape[2]
    sb_heads, sb_dim = cache_k.shape[3], cache_k.shape[4]
    sb_w = sb_heads * sb_dim
    ret_heads, ret_dk, ret_dv = state_ret.shape[2], state_ret.shape[3], state_ret.shape[4]
    ret_qk_w, ret_v_w = ret_heads * ret_dk, ret_heads * ret_dv
    ssm_heads, ssm_hdim, ssm_dstate = state_ssm.shape[2], state_ssm.shape[3], state_ssm.shape[4]
    d_inner = ssm_heads * ssm_hdim
    conv_dim = state_ssm_conv.shape[-1]
    groups = (conv_dim - d_inner) // (2 * ssm_dstate)
    hpg = ssm_heads // groups
    ssm_width = conv_w_ssd.shape[1]
    gs = groups * ssm_dstate
    main_w = d_inner + conv_dim

    w_in_ab_b, w_out_ab_b = w_in_ab.astype(BF16), w_out_ab.astype(BF16)
    w_in_ssd_b, w_out_ssd_b = w_in_ssd.astype(BF16), w_out_ssd.astype(BF16)
    w_dt_b = w_in_ssd_b[:, :, main_w:]
    w_up_b, w_down_b = w_up.astype(BF16), w_down.astype(BF16)

    xp_f = x_prompt.reshape(bp * seq, d_model)
    xs_f = x_sample.reshape(bs, d_model)
    xp_b = xp_f.astype(BF16)
    xs_b = xs_f.astype(BF16)

    outs = dict(k_p=[], v_p=[], k_s=[], v_s=[], ret_p=[], ret_s=[], ssm_p=[], ssm_s=[],
                sconv_p=[], sconv_s=[], fconv_p=[], fconv_s=[])
    for layer in range(depth):
        idx = layer // 2
        if layer % 2 == 0:
            q_off, k_off, v_off, r_off = 0, sb_w, 2 * sb_w, 3 * sb_w
            ret_w = 2 * ret_qk_w + 2 * ret_v_w
            rk_c, rv_c, rg_c = ret_qk_w, 2 * ret_qk_w, 2 * ret_qk_w + ret_v_w
            q = _matmul(xp_b, w_in_ab_b, idx, BF16, 1024, 1024, q_off, sb_w)
            k = _matmul(xp_b, w_in_ab_b, idx, F32, 1024, 1024, k_off, sb_w)
            v = _matmul(xp_b, w_in_ab_b, idx, F32, 1024, 1024, v_off, sb_w)
            ret = _matmul(xp_b, w_in_ab_b, idx, F32, 1024, 1024, r_off, ret_w)
            ya = _sb_prompt(q, k, v, sb_bias[idx], bp, seq, sb_heads)
            yr, s_new = _ret_prompt(ret, bp, seq, ret_heads, ret_dk, ret_dv, 0, rk_c, rv_c, rg_c)
            outs['k_p'].append(k.reshape(bp, seq, sb_heads, sb_dim))
            outs['v_p'].append(v.reshape(bp, seq, sb_heads, sb_dim))
            outs['ret_p'].append(s_new)
            xp_f, xp_b = _matmul_res_ln([ya, yr], w_out_ab_b, idx, xp_f, ln1_g[layer], ln1_b[layer], alpha,
                                        RES_LN_ROWS)
            proj = _matmul_small(xs_b, w_in_ab_b, idx, SMALL_TK)
            q, k, v = (proj[:, o:o + sb_w] for o in (q_off, k_off, v_off))
            ret = proj[:, r_off:]
            ya = _sb_decode(q.reshape(bs, sb_heads, sb_dim), sb_bias[idx], cache_k[idx], cache_v[idx],
                            page_table, 8)
            yr, s_new = _ret_decode(ret[:, :rk_c].reshape(bs, ret_heads, ret_dk),
                                    ret[:, rk_c:rv_c].reshape(bs, ret_heads, ret_dk),
                                    ret[:, rv_c:rg_c].reshape(bs, ret_heads, ret_dv),
                                    ret[:, rg_c:].reshape(bs, ret_heads, ret_dv),
                                    state_ret[idx], past_len, 4)
            outs['k_s'].append(k.reshape(bs, 1, sb_heads, sb_dim))
            outs['v_s'].append(v.reshape(bs, 1, sb_heads, sb_dim))
            outs['ret_s'].append(s_new)
            xs_f, xs_b = _matmul_res_ln([ya.reshape(bs, sb_w).astype(BF16), yr.reshape(bs, ret_v_w).astype(BF16)],
                                        w_out_ab_b, idx, xs_f, ln1_g[layer], ln1_b[layer], alpha, RES_LN_ROWS)
        else:
            tm = min(CONV_ROWS, seq)
            z = _matmul(xp_b, w_in_ssd_b, idx, F32, 1024, 1024, 0, d_inner)
            dt_raw = _matmul(xp_b, w_dt_b, idx, F32, 1024, ssm_heads)
            xbc, tails = _matmul_conv(xp_b, w_in_ssd_b, idx, (d_inner,), conv_w_ssd[idx], conv_b_ssd[idx], (0,),
                                      conv_dim, seq, F32, tm, 512, CONV_SUB_ROWS)
            y, h_new = _ssd_prompt(xbc, z, dt_raw, dt_bias[idx], a_log[idx], d_skip[idx], norm_w_ssd[idx],
                                   bp, seq, groups, hpg, ssm_hdim, ssm_dstate)
            outs['ssm_p'].append(h_new)
            outs['sconv_p'].append(_tile_tails(tails[0], bp, seq, tm, ssm_width - 1))
            xp_f, xp_b = _matmul_res_ln([y], w_out_ssd_b, idx, xp_f, ln1_g[layer], ln1_b[layer], alpha,
                                        RES_LN_ROWS)
            proj = _matmul_small(xs_b, w_in_ssd_b, idx, SMALL_TK, main_w)
            z, xbc_raw = proj[:, :d_inner], proj[:, d_inner:]
            dt_raw = _matmul(xs_b, w_dt_b, idx, F32, 128, ssm_heads)
            xbc = _conv_decode(xbc_raw, jnp.swapaxes(state_ssm_conv[idx], 0, 1), conv_w_ssd[idx],
                               conv_b_ssd[idx], False, F32, 1024)
            dt, dec = _dt_decode(dt_raw, dt_bias[idx], a_log[idx])
            y, h_new = _ssd_decode(xbc[:, :d_inner], xbc[:, d_inner:d_inner + gs], xbc[:, d_inner + gs:],
                                   z, dt, dec, d_skip[idx], norm_w_ssd[idx], state_ssm[idx],
                                   groups, hpg, ssm_hdim, 2)
            outs['ssm_s'].append(h_new)
            outs['sconv_s'].append(jnp.concatenate([state_ssm_conv[idx][:, 1:], xbc_raw[:, None]], axis=1))
            xs_f, xs_b = _matmul_res_ln([y.astype(BF16)], w_out_ssd_b, idx, xs_f, ln1_g[layer], ln1_b[layer],
                                        alpha, RES_LN_ROWS)

        xp_f, xp_b, fc = _ffn(xp_f, xp_b, None, w_up_b, w_down_b, layer, conv_w_ffn[layer], conv_b_ffn[layer],
                              ln2_g[layer], ln2_b[layer], alpha, seq)
        outs['fconv_p'].append(fc)
        xs_f, xs_b, fc = _ffn(xs_f, xs_b, state_ffn_conv[layer], w_up_b, w_down_b, layer, conv_w_ffn[layer],
                              conv_b_ffn[layer], ln2_g[layer], ln2_b[layer], alpha, 1)
        outs['fconv_s'].append(fc)

    st = lambda key: outs[key][0][None] if len(outs[key]) == 1 else jnp.stack(outs[key])
    return (xp_f.reshape(bp, seq, d_model), xs_f.reshape(bs, 1, d_model),
            st('k_p'), st('v_p'), st('k_s'), st('v_s'), st('ret_p'), st('ret_s'),
            st('ssm_p'), st('ssm_s'), st('sconv_p'), st('sconv_s'), st('fconv_p'), st('fconv_s'))
```

```python
import functools
import math

import jax
import jax.numpy as jnp
from jax import lax
from jax.experimental import pallas as pl
from jax.experimental.pallas import tpu as pltpu

F32 = jnp.float32
BF16 = jnp.bfloat16

LANES = 128
SUBLANES = 8
VMEM_LIMIT_BYTES = 52 * 1024 * 1024

CHUNK = 128
LN_EPS = 1e-5
NORM_EPS = 1e-5
ROPE_BASE = 10000.0

NT_DIMS = (((1,), (1,)), ((), ()))
TN_DIMS = (((0,), (0,)), ((), ()))


def _params(*semantics):
    return pltpu.CompilerParams(dimension_semantics=semantics, vmem_limit_bytes=VMEM_LIMIT_BYTES)


def _softplus(z):
    return jnp.maximum(z, 0.0) + jnp.log(1.0 + jnp.exp(-jnp.abs(z)))


LOG2E = math.log2(math.e)


def _softplus_bits(z2):
    return jnp.maximum(z2, 0.0) + jnp.log2(1.0 + jnp.exp2(-jnp.abs(z2)))


def _silu(x):
    return x * jax.nn.sigmoid(x)


def _dot(a, b, dims=None):
    if dims is None:
        return jnp.dot(a, b, preferred_element_type=F32)
    return lax.dot_general(a, b, dims, preferred_element_type=F32)


def _dot_split(a, m, dims=None, terms=3):
    out = None
    rem = a
    for t in range(terms):
        part = rem.astype(BF16)
        if t + 1 < terms:
            rem = rem - part.astype(F32)
        out_t = _dot(part, m, dims)
        out = out_t if out is None else out + out_t
    return out


def _mm_kernel(x_ref, w_ref, o_ref):
    o_ref[...] = _dot(x_ref[...], w_ref[...]).astype(o_ref.dtype)


def _matmul(x, w, layer, out_dtype, tm, tn, col0=0, ncols=None):
    m, k = x.shape
    n = w.shape[2] - col0 if ncols is None else ncols
    tm, tn = min(tm, m), min(tn, n)
    assert m % tm == 0 and n % tn == 0 and col0 % tn == 0
    c0 = col0 // tn
    return pl.pallas_call(
        _mm_kernel,
        grid=(n // tn, m // tm),
        in_specs=[pl.BlockSpec((tm, k), lambda j, i: (i, 0)),
                  pl.BlockSpec((None, k, tn), lambda j, i: (layer, 0, c0 + j))],
        out_specs=pl.BlockSpec((tm, tn), lambda j, i: (i, j)),
        out_shape=jax.ShapeDtypeStruct((m, n), out_dtype),
        compiler_params=_params("parallel", "parallel"),
        name="matmul",
    )(x, w)


def _mm_small_kernel(x_ref, w_ref, o_ref):
    kk = pl.program_id(0)
    tk = w_ref.shape[0]
    start = pl.multiple_of(kk * tk, tk)
    part = _dot(x_ref[:, pl.ds(start, tk)], w_ref[...])

    @pl.when(kk == 0)
    def _():
        o_ref[...] = part

    @pl.when(kk > 0)
    def _():
        o_ref[...] += part


def _matmul_small(x, w, layer, tk, ncols=None):
    m, k = x.shape
    n = w.shape[2] if ncols is None else ncols
    assert k % tk == 0
    return pl.pallas_call(
        _mm_small_kernel,
        grid=(k // tk,),
        in_specs=[pl.BlockSpec((m, k), lambda kk: (0, 0)),
                  pl.BlockSpec((None, tk, n), lambda kk: (layer, kk, 0))],
        out_specs=pl.BlockSpec((m, n), lambda kk: (0, 0)),
        out_shape=jax.ShapeDtypeStruct((m, n), F32),
        compiler_params=_params("arbitrary"),
        name="matmul_small",
    )(x, w)


def _mm_res_ln_kernel(*refs, widths, alpha):
    nseg = len(widths)
    a_refs = refs[:nseg]
    w_ref, r_ref, g_ref, b_ref, of_ref, ob_ref = refs[nseg:]
    y = alpha * r_ref[...]
    lo = 0
    for a_ref, width in zip(a_refs, widths):
        y = y + _dot(a_ref[...], w_ref[lo:lo + width, :])
        lo += width
    mu = jnp.mean(y, axis=-1, keepdims=True)
    d = y - mu
    var = jnp.mean(d * d, axis=-1, keepdims=True)
    out = d * lax.rsqrt(var + LN_EPS) * g_ref[...] + b_ref[...]
    of_ref[...] = out
    ob_ref[...] = out.astype(BF16)


def _matmul_res_ln(acts, w, layer, resid, gain, bias, alpha, tm):
    m = acts[0].shape[0]
    _, k, d = w.shape
    tm = min(tm, m)
    widths = tuple(a.shape[1] for a in acts)
    assert m % tm == 0 and sum(widths) == k
    return pl.pallas_call(
        functools.partial(_mm_res_ln_kernel, widths=widths, alpha=alpha),
        grid=(m // tm,),
        in_specs=[pl.BlockSpec((tm, width), lambda i: (i, 0)) for width in widths] + [
                  pl.BlockSpec((None, k, d), lambda i: (layer, 0, 0), pipeline_mode=pl.Buffered(1)),
                  pl.BlockSpec((tm, d), lambda i: (i, 0)),
                  pl.BlockSpec((1, d), lambda i: (0, 0)),
                  pl.BlockSpec((1, d), lambda i: (0, 0))],
        out_specs=[pl.BlockSpec((tm, d), lambda i: (i, 0)),
                   pl.BlockSpec((tm, d), lambda i: (i, 0))],
        out_shape=[jax.ShapeDtypeStruct((m, d), F32), jax.ShapeDtypeStruct((m, d), BF16)],
        compiler_params=_params("parallel"),
        name="matmul_res_ln",
    )(*acts, w, resid, gain.reshape(1, d), bias.reshape(1, d))


def _sb_prompt_kernel(bias_ref, q_ref, k_ref, v_ref, tri_ref, o_ref, *, scale, tile):
    h = pl.program_id(1)
    qi = pl.program_id(2)
    bias2 = bias_ref[h] * LOG2E
    q = q_ref[...].astype(BF16)
    tri = tri_ref[...]
    row = lax.broadcasted_iota(jnp.int32, (tile, tile), 0)
    col = lax.broadcasted_iota(jnp.int32, (tile, tile), 1)
    valid = col < row
    nsub = tile // CHUNK

    def group(g, carry, acc, diagonal):
        start = pl.multiple_of(g * tile, tile)
        k = k_ref[pl.ds(start, tile), :].astype(BF16)
        v = v_ref[pl.ds(start, tile), :].astype(BF16)
        z2 = _dot(q, k, NT_DIMS) * (scale * LOG2E) + bias2
        sp = _softplus_bits(z2)
        fail = jnp.where(valid, sp, 0.0) if diagonal else sp
        hi = fail.astype(BF16)
        lo = (fail - hi.astype(F32)).astype(BF16)
        later = [None] * nsub
        for c in range(nsub - 1, -1, -1):
            cols = slice(c * CHUNK, (c + 1) * CHUNK)
            sums = _dot(jnp.concatenate([hi[:, cols], lo[:, cols]], axis=1), tri)
            later[c] = sums[:, :CHUNK] + carry
            carry = carry + sums[:, CHUNK:]
        w = jnp.exp2(z2 - sp + jnp.concatenate(later, axis=1))
        if diagonal:
            w = jnp.where(valid, w, 0.0)
        acc = acc + _dot(w.astype(BF16), v)
        return carry, acc

    zeros = jnp.zeros((tile, CHUNK), F32)
    carry, acc = group(qi, zeros, zeros, True)

    def body(i, c):
        return group(qi - 1 - i, c[0], c[1], False)

    carry, acc = lax.fori_loop(0, qi, body, (carry, acc))
    o_ref[...] = acc.astype(o_ref.dtype)


def _sb_prompt(q, k, v, bias, bsz, length, heads, tile=512):
    tile = min(tile, length)
    nq = length // tile
    j = lax.broadcasted_iota(jnp.int32, (CHUNK, CHUNK), 0)
    s = lax.broadcasted_iota(jnp.int32, (CHUNK, CHUNK), 1)
    tri = -jnp.concatenate([(j > s).astype(BF16), jnp.ones((CHUNK, CHUNK), BF16)], axis=1)
    tri = jnp.concatenate([tri, tri], axis=0)
    return pl.pallas_call(
        functools.partial(_sb_prompt_kernel, scale=CHUNK ** -0.5, tile=tile),
        grid=(bsz, heads, nq),
        in_specs=[pl.BlockSpec(memory_space=pltpu.SMEM),
                  pl.BlockSpec((tile, CHUNK), lambda b, h, i: (b * nq + i, h)),
                  pl.BlockSpec((length, CHUNK), lambda b, h, i: (b, h)),
                  pl.BlockSpec((length, CHUNK), lambda b, h, i: (b, h)),
                  pl.BlockSpec((2 * CHUNK, 2 * CHUNK), lambda b, h, i: (0, 0))],
        out_specs=pl.BlockSpec((tile, CHUNK), lambda b, h, i: (b * nq + i, h)),
        out_shape=jax.ShapeDtypeStruct((bsz * length, heads * CHUNK), BF16),
        compiler_params=_params("parallel", "parallel", "arbitrary"),
        name="sb_prompt",
    )(bias, q, k, v, tri)


def _rotate(x, cos, sin_signed):
    return x * cos + pltpu.roll(x, x.shape[-1] // 2, x.ndim - 1) * sin_signed


def _ret_prompt_kernel(q_ref, k_ref, v_ref, g_ref, cos_ref, sin_ref, dec_ref, te_ref, fs_ref, cd_ref,
                       y_ref, so_ref, s_ref, *, nchunks, kscale, dk, dv):
    c = pl.program_id(2)

    @pl.when(c == 0)
    def _():
        s_ref[...] = jnp.zeros_like(s_ref)

    cos = cos_ref[...]
    sin = sin_ref[...]
    for h in range(s_ref.shape[0]):
        qk = slice(h * dk, (h + 1) * dk)
        vg = slice(h * dv, (h + 1) * dv)
        q = _rotate(q_ref[:, qk], cos, sin)
        k = _rotate(k_ref[:, qk], cos, sin) * kscale
        v = v_ref[:, vg].astype(BF16)
        scores = _dot(q.astype(BF16), k.astype(BF16), NT_DIMS) * dec_ref[h]
        out = _dot(scores.astype(BF16), v)
        state = s_ref[h]
        out = out + _dot((q * fs_ref[h]).astype(BF16), state.astype(BF16))
        state = state * cd_ref[h] + _dot((k * te_ref[h]).astype(BF16), v, TN_DIMS)
        s_ref[h] = state
        ms = jnp.mean(out * out, axis=-1, keepdims=True)
        y = out * lax.rsqrt(ms + NORM_EPS) * _silu(g_ref[:, vg])
        y_ref[:, vg] = y.astype(y_ref.dtype)

    @pl.when(c == nchunks - 1)
    def _():
        so_ref[0] = s_ref[...]


def _ret_consts(heads, dk):
    lg = jnp.log(1.0 - 2.0 ** (-5.0 - jnp.arange(heads, dtype=F32)))
    idx = jnp.arange(CHUNK, dtype=F32)
    gap = idx[:, None] - idx[None, :]
    dec = jnp.where(gap >= 0, jnp.exp(lg[:, None, None] * jnp.maximum(gap, 0.0)), 0.0)
    te = jnp.exp(lg[:, None] * (CHUNK - 1.0 - idx)[None, :])
    fs = jnp.exp(lg[:, None] * (idx + 1.0)[None, :])
    rep = lambda a: jnp.broadcast_to(a[:, :, None], (heads, CHUNK, dk))
    return lg, dec, rep(te), rep(fs)


def _rope_tables(pos, dk):
    half = dk // 2
    inv_freq = ROPE_BASE ** (-jnp.linspace(0.0, 1.0, half, dtype=F32))
    ang = pos.astype(F32)[:, None] * inv_freq[None, :]
    cos, sin = jnp.cos(ang), jnp.sin(ang)
    return jnp.concatenate([cos, cos], axis=-1), jnp.concatenate([-sin, sin], axis=-1)


def _ret_prompt(proj, bsz, length, heads, dk, dv, q_col, k_col, v_col, g_col, hb=4):
    nc = length // CHUNK
    assert heads % hb == 0 and all(c % (hb * dk) == 0 for c in (q_col, k_col))
    assert all(c % (hb * dv) == 0 for c in (v_col, g_col))
    lg, dec, te, fs = _ret_consts(heads, dk)
    cd = jnp.broadcast_to(jnp.exp(lg * CHUNK)[:, None, None], (heads, 1, dv))
    cos, sin = _rope_tables(jnp.arange(length), dk)
    row = lambda b, c: b * nc + c
    qk_spec = lambda col: pl.BlockSpec((CHUNK, hb * dk), lambda b, h, c: (row(b, c), col // (hb * dk) + h))
    vg_spec = lambda col: pl.BlockSpec((CHUNK, hb * dv), lambda b, h, c: (row(b, c), col // (hb * dv) + h))
    head_spec = lambda r, w: pl.BlockSpec((hb, r, w), lambda b, h, c: (h, 0, 0))
    return pl.pallas_call(
        functools.partial(_ret_prompt_kernel, nchunks=nc, kscale=dk ** -0.5, dk=dk, dv=dv),
        grid=(bsz, heads // hb, nc),
        in_specs=[qk_spec(q_col), qk_spec(k_col), vg_spec(v_col), vg_spec(g_col),
                  pl.BlockSpec((CHUNK, dk), lambda b, h, c: (c, 0)),
                  pl.BlockSpec((CHUNK, dk), lambda b, h, c: (c, 0)),
                  head_spec(CHUNK, CHUNK), head_spec(CHUNK, dk), head_spec(CHUNK, dk), head_spec(1, dv)],
        out_specs=[pl.BlockSpec((CHUNK, hb * dv), lambda b, h, c: (row(b, c), h)),
                   pl.BlockSpec((1, hb, dk, dv), lambda b, h, c: (b, h, 0, 0))],
        out_shape=[jax.ShapeDtypeStruct((bsz * length, heads * dv), BF16),
                   jax.ShapeDtypeStruct((bsz, heads, dk, dv), F32)],
        scratch_shapes=[pltpu.VMEM((hb, dk, dv), F32)],
        compiler_params=_params("parallel", "parallel", "arbitrary"),
        name="ret_prompt",
    )(proj, proj, proj, proj, cos, sin, dec, te, fs, cd)


HALO_ROWS = 16
RES_LN_ROWS = 256
CONV_ROWS = 2048
CONV_SUB_ROWS = 1024


def _mm_conv_kernel(x_ref, xh_ref, *refs, width, tiles_per_seq, nbranch, sub):
    w_refs = refs[0:nbranch]
    cw_refs = refs[nbranch:2 * nbranch]
    cb_refs = refs[2 * nbranch:3 * nbranch]
    o_ref = refs[3 * nbranch]
    tail_refs = refs[3 * nbranch + 1:4 * nbranch + 1]
    ext_refs = refs[4 * nbranch + 1:]
    tm = x_ref.shape[0]
    at_start = (pl.program_id(1) % tiles_per_seq) == 0
    xh = xh_ref[...]
    for br in range(nbranch):
        before = _dot(xh, w_refs[br][...])[HALO_ROWS - SUBLANES:]
        ext_refs[br][0:SUBLANES, :] = jnp.where(at_start, 0.0, before)
    for r in range(tm // sub):
        rows = slice(r * sub, (r + 1) * sub)
        base = SUBLANES + r * sub
        conv = []
        for br in range(nbranch):
            h = _dot(x_ref[rows, :], w_refs[br][...])
            ext_refs[br][base:base + sub, :] = h
            acc = cb_refs[br][...] + h * cw_refs[br][width - 1:width, :]
            for t in range(width - 1):
                back = width - 1 - t
                acc = acc + ext_refs[br][base - back:base - back + sub, :] * cw_refs[br][t:t + 1, :]
            conv.append(acc)
        out = _silu(conv[0]) * conv[1] if nbranch == 2 else _silu(conv[0])
        o_ref[rows, :] = out.astype(o_ref.dtype)
    for br in range(nbranch):
        tail_refs[br][...] = ext_refs[br][tm:tm + SUBLANES, :]


def _matmul_conv(x, w, layer, w_cols, conv_w, conv_b, conv_cols, ncols, length, out_dtype, tm, tn, sub):
    m, k = x.shape
    width = conv_w.shape[0]
    nbranch = len(w_cols)
    assert m % tm == 0 and length % tm == 0 and ncols % tn == 0 and tm % HALO_ROWS == 0
    assert all(c % tn == 0 for c in list(w_cols) + list(conv_cols)) and width - 1 <= SUBLANES
    nj = ncols // tn
    halo_step = tm // HALO_ROWS
    cb2 = conv_b.reshape(1, -1)

    def col_spec(rows, c0):
        return pl.BlockSpec((rows, tn), lambda j, i: (0, c0 // tn + j))

    outs = pl.pallas_call(
        functools.partial(_mm_conv_kernel, width=width, tiles_per_seq=length // tm, nbranch=nbranch,
                          sub=min(sub, tm)),
        grid=(nj, m // tm),
        in_specs=[pl.BlockSpec((tm, k), lambda j, i: (i, 0)),
                  pl.BlockSpec((HALO_ROWS, k), lambda j, i: (jnp.maximum(i * halo_step - 1, 0), 0))]
                 + [pl.BlockSpec((None, k, tn), lambda j, i, c=c: (layer, 0, c // tn + j)) for c in w_cols]
                 + [col_spec(width, c) for c in conv_cols]
                 + [col_spec(1, c) for c in conv_cols],
        out_specs=[pl.BlockSpec((tm, tn), lambda j, i: (i, j))]
                  + [pl.BlockSpec((SUBLANES, tn), lambda j, i: (i, j))] * nbranch,
        out_shape=[jax.ShapeDtypeStruct((m, ncols), out_dtype)]
                  + [jax.ShapeDtypeStruct((m // tm * SUBLANES, ncols), F32)] * nbranch,
        scratch_shapes=[pltpu.VMEM((tm + SUBLANES, tn), F32)] * nbranch,
        compiler_params=_params("parallel", "arbitrary"),
        name="matmul_conv",
    )(x, x, *([w] * nbranch), *([conv_w] * nbranch), *([cb2] * nbranch))
    return outs[0], outs[1:]


def _tile_tails(tails, bsz, length, tm, keep):
    per_seq = length // tm
    t = tails.reshape(bsz * per_seq, SUBLANES, tails.shape[-1])
    return t[per_seq - 1::per_seq, SUBLANES - keep:, :]


def _ssd_prompt_kernel(x_ref, b_ref, c_ref, z_ref, dtc_ref, dtr_ref, dbc_ref, dbr_ref, alc_ref, alr_ref,
                       dsk_ref, nw_ref, ltri_ref, ones_ref, y_ref, ho_ref, h_ref, *, nchunks, hpg, hdim):
    c = pl.program_id(2)

    @pl.when(c == 0)
    def _():
        h_ref[...] = jnp.zeros_like(h_ref)

    ltri = ltri_ref[...]
    ones = ones_ref[...]
    row = lax.broadcasted_iota(jnp.int32, (CHUNK, CHUNK), 0)
    col = lax.broadcasted_iota(jnp.int32, (CHUNK, CHUNK), 1)
    causal = col <= row
    pair = LANES // hdim
    slabs = hpg // pair
    gw = hpg * hdim
    dstate = b_ref.shape[1] // dtc_ref.shape[0]
    for gi in range(dtc_ref.shape[0]):
        gcols = slice(gi * gw, (gi + 1) * gw)
        scols = slice(gi * dstate, (gi + 1) * dstate)
        dtc = _softplus(dtc_ref[gi] + dbc_ref[gi])
        dtr = _softplus(dtr_ref[gi] + dbr_ref[gi])
        dac = dtc * (-jnp.exp(alc_ref[gi]))
        dar = dtr * (-jnp.exp(alr_ref[gi]))
        cumc = _dot_split_lhs01(ltri, dac)
        cumr = _dot_split(dar, ltri, NT_DIMS)
        totr = _dot_split(dar, ones)
        to_end = jnp.exp(totr - cumr) * dtr
        chunk_decay = jnp.exp(totr)

        bmat = b_ref[:, scols].astype(BF16)
        cmat = c_ref[:, scols]
        cb = _dot(cmat.astype(BF16), bmat, NT_DIMS)
        x = x_ref[:, gcols]
        ys = []
        for p in range(slabs):
            xp = x[:, p * LANES:(p + 1) * LANES]
            xpb = xp.astype(BF16)
            hp = h_ref[gi * slabs + p]
            hpb = hp.astype(BF16)
            yp = None
            te = None
            cd = None
            for e_local in range(pair):
                e = p * pair + e_local
                cum_i = jnp.broadcast_to(cumc[:, e:e + 1], (CHUNK, CHUNK))
                seg = cum_i - cumr[e:e + 1, :]
                decay = jnp.exp(jnp.where(causal, seg, -jnp.inf))
                w = (cb * decay * dtr[e:e + 1, :]).astype(BF16)
                y_e = _dot(w, xpb) + _dot((cmat * jnp.exp(cum_i)).astype(BF16), hpb, NT_DIMS)
                te_e = jnp.broadcast_to(to_end[e:e + 1, :], (CHUNK, CHUNK))
                cd_e = jnp.broadcast_to(chunk_decay[e:e + 1, :], (CHUNK, CHUNK))
                if e_local == 0:
                    yp, te, cd = y_e, te_e, cd_e
                else:
                    in_head = (col >= e_local * hdim) & (col < (e_local + 1) * hdim)
                    in_rows = (row >= e_local * hdim) & (row < (e_local + 1) * hdim)
                    yp = jnp.where(in_head, y_e, yp)
                    te = jnp.where(in_rows, te_e, te)
                    cd = jnp.where(in_rows, cd_e, cd)
            h_ref[gi * slabs + p] = hp * cd + _dot((xp.T * te).astype(BF16), bmat)
            ys.append(yp)
        y = jnp.concatenate(ys, axis=1) + dsk_ref[:, gcols] * x
        y = y * _silu(z_ref[:, gcols])
        ms = jnp.mean(y * y, axis=-1, keepdims=True)
        y = y * lax.rsqrt(ms + NORM_EPS) * nw_ref[:, gcols]
        y_ref[:, gcols] = y.astype(y_ref.dtype)

    @pl.when(c == nchunks - 1)
    def _():
        ho_ref[0] = h_ref[...]


def _dot_split_lhs01(m, a, terms=3):
    out = None
    rem = a
    for t in range(terms):
        part = rem.astype(BF16)
        if t + 1 < terms:
            rem = rem - part.astype(F32)
        out_t = _dot(m, part)
        out = out_t if out is None else out + out_t
    return out


def _ssd_prompt(xbc, zproj, dt_raw, dt_bias, a_log, d_skip, norm_w, bsz, length, groups, hpg, hdim, dstate, gb=4):
    nc = length // CHUNK
    gw = hpg * hdim
    d_inner = groups * gw
    assert dstate == CHUNK and gw % LANES == 0 and groups % gb == 0 and d_inner % (gb * dstate) == 0
    heads = groups * hpg
    rows = bsz * length
    dt3 = dt_raw.reshape(bsz, length, groups, hpg)
    dtc = jnp.transpose(dt3, (2, 0, 1, 3)).reshape(groups, rows, hpg)
    dtr = jnp.transpose(dt3, (0, 2, 3, 1)).reshape(bsz * groups, hpg, length)
    i = lax.broadcasted_iota(jnp.int32, (CHUNK, CHUNK), 0)
    j = lax.broadcasted_iota(jnp.int32, (CHUNK, CHUNK), 1)
    ltri = (j <= i).astype(BF16)
    ones = jnp.ones((CHUNK, CHUNK), BF16)
    slabs = gw // LANES
    ng = groups // gb
    b_blk0 = d_inner // (gb * dstate)
    c_blk0 = b_blk0 + ng
    row = lambda b, c: b * nc + c
    wide = pl.BlockSpec((CHUNK, gb * gw), lambda b, g, c: (row(b, c), g))
    par = lambda r, w: pl.BlockSpec((gb, r, w), lambda b, g, c: (g, 0, 0))
    vec = pl.BlockSpec((1, gb * gw), lambda b, g, c: (0, g))
    const = pl.BlockSpec((CHUNK, CHUNK), lambda b, g, c: (0, 0))
    y, h = pl.pallas_call(
        functools.partial(_ssd_prompt_kernel, nchunks=nc, hpg=hpg, hdim=hdim),
        grid=(bsz, ng, nc),
        in_specs=[wide,
                  pl.BlockSpec((CHUNK, gb * dstate), lambda b, g, c: (row(b, c), b_blk0 + g)),
                  pl.BlockSpec((CHUNK, gb * dstate), lambda b, g, c: (row(b, c), c_blk0 + g)),
                  wide,
                  pl.BlockSpec((gb, CHUNK, hpg), lambda b, g, c: (g, row(b, c), 0)),
                  pl.BlockSpec((gb, hpg, CHUNK), lambda b, g, c: (b * ng + g, 0, c)),
                  par(1, hpg), par(hpg, 1), par(1, hpg), par(hpg, 1), vec, vec, const, const],
        out_specs=[wide,
                   pl.BlockSpec((1, gb * slabs, LANES, dstate), lambda b, g, c: (b, g, 0, 0))],
        out_shape=[jax.ShapeDtypeStruct((rows, d_inner), BF16),
                   jax.ShapeDtypeStruct((bsz, groups * slabs, LANES, dstate), F32)],
        scratch_shapes=[pltpu.VMEM((gb * slabs, LANES, dstate), F32)],
        compiler_params=_params("parallel", "parallel", "arbitrary"),
        name="ssd_prompt",
    )(xbc, xbc, xbc, zproj, dtc, dtr,
      dt_bias.reshape(groups, 1, hpg), dt_bias.reshape(groups, hpg, 1),
      a_log.reshape(groups, 1, hpg), a_log.reshape(groups, hpg, 1),
      jnp.repeat(d_skip, hdim).reshape(1, d_inner), norm_w.reshape(1, d_inner), ltri, ones)
    return y, h.reshape(bsz, heads, hdim, dstate)


def _sb_decode_kernel(pt_ref, q_ref, bias_ref, ones_ref, *refs, pages_per_step, page, scale):
    k_refs = refs[:pages_per_step]
    v_refs = refs[pages_per_step:2 * pages_per_step]
    o_ref, carry_ref, acc_ref = refs[2 * pages_per_step:]
    g = pl.program_id(1)

    @pl.when(g == 0)
    def _():
        carry_ref[...] = jnp.ones_like(carry_ref)
        acc_ref[...] = jnp.zeros_like(acc_ref)

    q2 = q_ref[0] * (scale * LOG2E)
    bias2 = bias_ref[...]
    ones = ones_ref[...]
    heads, hd = q2.shape
    seg = 16
    nseg = page // seg
    carry = carry_ref[...]
    acc = acc_ref[...]
    for i in range(pages_per_step):
        kk = k_refs[i][0]
        vv = v_refs[i][0]
        prod = (kk * q2[None]).reshape(page * heads, hd)
        z2 = _dot(prod.astype(BF16), ones).reshape(page, heads, hd) + bias2[None]
        fail = 1.0 / (1.0 + jnp.exp2(z2))
        beta = 1.0 - fail
        parts = [None] * nseg
        totals = [None] * nseg
        for s in range(nseg):
            run = None
            part = None
            for t in range((s + 1) * seg - 1, s * seg - 1, -1):
                weight = beta[t] if run is None else beta[t] * run
                term = weight * vv[t]
                part = term if part is None else part + term
                run = fail[t] if run is None else run * fail[t]
            parts[s], totals[s] = part, run
        for s in range(nseg - 1, -1, -1):
            acc = acc + carry * parts[s]
            carry = carry * totals[s]
    carry_ref[...] = carry
    acc_ref[...] = acc
    o_ref[0] = acc


def _sb_decode(q, bias, cache_k, cache_v, page_table, pages_per_step):
    bsz, heads, hd = q.shape
    page = cache_k.shape[1]
    n_pages = page_table.shape[1]
    assert n_pages % pages_per_step == 0
    steps = n_pages // pages_per_step

    def page_map(i):
        return lambda b, g, pt: (pt[b, n_pages - 1 - (g * pages_per_step + i)], 0, 0, 0)

    kv_specs = [pl.BlockSpec((1, page, heads, hd), page_map(i)) for i in range(pages_per_step)]
    grid_spec = pltpu.PrefetchScalarGridSpec(
        num_scalar_prefetch=1,
        grid=(bsz, steps),
        in_specs=[pl.BlockSpec((1, heads, hd), lambda b, g, pt: (b, 0, 0)),
                  pl.BlockSpec((heads, hd), lambda b, g, pt: (0, 0)),
                  pl.BlockSpec((hd, hd), lambda b, g, pt: (0, 0))] + kv_specs + kv_specs,
        out_specs=pl.BlockSpec((1, heads, hd), lambda b, g, pt: (b, 0, 0)),
        scratch_shapes=[pltpu.VMEM((heads, hd), F32), pltpu.VMEM((heads, hd), F32)],
    )
    bias_rep = jnp.broadcast_to((bias.astype(F32) * LOG2E)[:, None], (heads, hd))
    ones = jnp.ones((hd, hd), BF16)
    return pl.pallas_call(
        functools.partial(_sb_decode_kernel, pages_per_step=pages_per_step, page=page, scale=hd ** -0.5),
        grid_spec=grid_spec,
        out_shape=jax.ShapeDtypeStruct((bsz, heads, hd), F32),
        compiler_params=_params("parallel", "arbitrary"),
        name="sb_decode",
    )(page_table, q, bias_rep, ones, *([cache_k] * pages_per_step), *([cache_v] * pages_per_step))


def _ret_decode_kernel(q_ref, k_ref, v_ref, g_ref, cos_ref, sin_ref, s_ref, y_ref, so_ref, *, gammas, kscale):
    bb, heads, dk = q_ref.shape
    cos = cos_ref[...]
    sin = sin_ref[...]
    pad = jnp.zeros((dk - 2 * heads, dk), F32)
    for bi in range(bb):
        q = _rotate(q_ref[bi], cos, sin)
        k = _rotate(k_ref[bi], cos, sin) * kscale
        qk_t = jnp.concatenate([q, k, pad], axis=0).T
        for h in range(heads):
            q_col = qk_t[:, h:h + 1]
            k_col = qk_t[:, heads + h:heads + h + 1]
            state = s_ref[bi, h] * gammas[h] + k_col * v_ref[bi, h:h + 1, :]
            so_ref[bi, h] = state
            out = jnp.sum(q_col * state, axis=0, keepdims=True)
            ms = jnp.mean(out * out, axis=-1, keepdims=True)
            y_ref[bi, h:h + 1, :] = out * lax.rsqrt(ms + NORM_EPS) * _silu(g_ref[bi, h:h + 1, :])


def _ret_decode(q, k, v, gate, state, pos, bb):
    bsz, heads, dk = q.shape
    dv = v.shape[-1]
    gammas = tuple(1.0 - 2.0 ** (-5.0 - h) for h in range(heads))
    cos, sin = _rope_tables(jnp.full((1,), pos), dk)
    return pl.pallas_call(
        functools.partial(_ret_decode_kernel, gammas=gammas, kscale=dk ** -0.5),
        grid=(bsz // bb,),
        in_specs=[pl.BlockSpec((bb, heads, dk), lambda i: (i, 0, 0)),
                  pl.BlockSpec((bb, heads, dk), lambda i: (i, 0, 0)),
                  pl.BlockSpec((bb, heads, dv), lambda i: (i, 0, 0)),
                  pl.BlockSpec((bb, heads, dv), lambda i: (i, 0, 0)),
                  pl.BlockSpec((1, dk), lambda i: (0, 0)),
                  pl.BlockSpec((1, dk), lambda i: (0, 0)),
                  pl.BlockSpec((bb, heads, dk, dv), lambda i: (i, 0, 0, 0))],
        out_specs=[pl.BlockSpec((bb, heads, dv), lambda i: (i, 0, 0)),
                   pl.BlockSpec((bb, heads, dk, dv), lambda i: (i, 0, 0, 0))],
        out_shape=[jax.ShapeDtypeStruct((bsz, heads, dv), F32),
                   jax.ShapeDtypeStruct((bsz, heads, dk, dv), F32)],
        compiler_params=_params("parallel"),
        name="ret_decode",
    )(q, k, v, gate, cos, sin, state)


def _dt_decode_kernel(dt_ref, bias_ref, alog_ref, dt_out_ref, dec_out_ref):
    dt = _softplus(dt_ref[...] + bias_ref[...])
    dt_out_ref[...] = dt
    dec_out_ref[...] = jnp.exp(dt * (-jnp.exp(alog_ref[...])))


def _dt_decode(dt_raw, dt_bias, a_log):
    bsz, heads = dt_raw.shape
    return pl.pallas_call(
        _dt_decode_kernel,
        out_shape=[jax.ShapeDtypeStruct((bsz, heads), F32), jax.ShapeDtypeStruct((bsz, heads), F32)],
        name="dt_decode",
    )(dt_raw, dt_bias.reshape(1, heads), a_log.reshape(1, heads))


def _ssd_decode_kernel(dec_ref, x_ref, dt_ref, z_ref, b_ref, c_ref, dsk_ref, nw_ref, h_ref, y_ref, ho_ref,
                       *, groups, hpg, hdim):
    bb = x_ref.shape[0]
    gw = hpg * hdim
    dstate = b_ref.shape[-1] // groups
    first = lax.broadcasted_iota(jnp.int32, (SUBLANES, gw), 0) == 0
    first_s = lax.broadcasted_iota(jnp.int32, (SUBLANES, dstate), 0) == 0
    b0 = pl.program_id(0) * bb
    for bi in range(bb):
        for g in range(groups):
            cols = slice(g * gw, (g + 1) * gw)
            scols = slice(g * dstate, (g + 1) * dstate)
            x = x_ref[bi, :, cols]
            dtx = x * dt_ref[bi, :, cols]
            a8 = jnp.where(first, jnp.broadcast_to(dtx, (SUBLANES, gw)), 0.0).astype(BF16)
            b8 = jnp.where(first_s, jnp.broadcast_to(b_ref[bi, :, scols], (SUBLANES, dstate)), 0.0).astype(BF16)
            c8 = jnp.where(first_s, jnp.broadcast_to(c_ref[bi, :, scols], (SUBLANES, dstate)), 0.0).astype(BF16)
            contrib = _dot(a8, b8, TN_DIMS)
            new = []
            for e in range(hpg):
                rows = slice(g * gw + e * hdim, g * gw + (e + 1) * hdim)
                h_new = h_ref[bi, rows, :] * dec_ref[b0 + bi, g * hpg + e] + contrib[e * hdim:(e + 1) * hdim]
                ho_ref[bi, rows, :] = h_new
                new.append(h_new)
            h_g = jnp.concatenate(new, axis=0).astype(BF16)
            y = _dot(c8, h_g, NT_DIMS)[0:1] + dsk_ref[:, cols] * x
            y = y * _silu(z_ref[bi, :, cols])
            ms = jnp.mean(y * y, axis=-1, keepdims=True)
            y_ref[bi, :, cols] = y * lax.rsqrt(ms + NORM_EPS) * nw_ref[:, cols]


def _ssd_decode(xs, bm, cm, z, dt, dec, d_skip, norm_w, state, groups, hpg, hdim, bb):
    bsz, d_inner = xs.shape
    heads = groups * hpg
    dstate = state.shape[-1]
    gs = groups * dstate
    r3 = lambda a: a.reshape(bsz, 1, a.shape[-1])
    dt_rep = jnp.repeat(dt, hdim, axis=1)
    row3 = lambda w: pl.BlockSpec((bb, 1, w), lambda i: (i, 0, 0))
    y, h = pl.pallas_call(
        functools.partial(_ssd_decode_kernel, groups=groups, hpg=hpg, hdim=hdim),
        grid=(bsz // bb,),
        in_specs=[pl.BlockSpec(memory_space=pltpu.SMEM),
                  row3(d_inner), row3(d_inner), row3(d_inner), row3(gs), row3(gs),
                  pl.BlockSpec((1, d_inner), lambda i: (0, 0)),
                  pl.BlockSpec((1, d_inner), lambda i: (0, 0)),
                  pl.BlockSpec((bb, heads * hdim, dstate), lambda i: (i, 0, 0))],
        out_specs=[row3(d_inner),
                   pl.BlockSpec((bb, heads * hdim, dstate), lambda i: (i, 0, 0))],
        out_shape=[jax.ShapeDtypeStruct((bsz, 1, d_inner), F32),
                   jax.ShapeDtypeStruct((bsz, heads * hdim, dstate), F32)],
        compiler_params=_params("parallel"),
        name="ssd_decode",
    )(dec, r3(xs), r3(dt_rep), r3(z), r3(bm), r3(cm),
      jnp.repeat(d_skip, hdim).reshape(1, d_inner), norm_w.reshape(1, d_inner),
      state.reshape(bsz, heads * hdim, dstate))
    return y.reshape(bsz, d_inner), h.reshape(bsz, heads, hdim, dstate)


def _conv_decode_kernel(*refs, width, nbranch):
    x_refs = refs[0:nbranch]
    st_refs = refs[nbranch:2 * nbranch]
    w_refs = refs[2 * nbranch:3 * nbranch]
    b_refs = refs[3 * nbranch:4 * nbranch]
    o_ref = refs[4 * nbranch]
    conv = []
    for x_ref, st_ref, w_ref, b_ref in zip(x_refs, st_refs, w_refs, b_refs):
        out = b_ref[...] + x_ref[...] * w_ref[width - 1:width, :]
        for t in range(width - 1):
            out = out + st_ref[t] * w_ref[t:t + 1, :]
        conv.append(out)
    out = _silu(conv[0]) * conv[1] if nbranch == 2 else _silu(conv[0])
    o_ref[...] = out.astype(o_ref.dtype)


def _conv_decode(x, state_t, w, b, gated, out_dtype, tc):
    bsz, c = x.shape
    width = w.shape[0]
    nbranch = 2 if gated else 1
    ncols = c // nbranch
    nj = ncols // tc
    offs = [br * nj for br in range(nbranch)]
    b2 = b.reshape(1, c)

    def specs(shape, lead):
        return [pl.BlockSpec(shape, lambda j, o=o: lead + (o + j,)) for o in offs]

    return pl.pallas_call(
        functools.partial(_conv_decode_kernel, width=width, nbranch=nbranch),
        grid=(nj,),
        in_specs=specs((bsz, tc), (0,)) + specs((width - 1, bsz, tc), (0, 0))
                 + specs((width, tc), (0,)) + specs((1, tc), (0,)),
        out_specs=pl.BlockSpec((bsz, tc), lambda j: (0, j)),
        out_shape=jax.ShapeDtypeStruct((bsz, ncols), out_dtype),
        compiler_params=_params("parallel"),
        name="conv_decode",
    )(*([x] * nbranch), *([state_t] * nbranch), *([w] * nbranch), *([b2] * nbranch))


SMALL_TK = 256


def _ffn(x_f32, x_bf16, conv_state, w_up, w_down, layer, conv_w, conv_b, ln_g, ln_b, alpha, length):
    rows = x_f32.shape[0]
    f = w_up.shape[2] // 2
    width = conv_w.shape[0]
    if conv_state is None:
        tm = min(CONV_ROWS, length)
        act, tails = _matmul_conv(x_bf16, w_up, layer, (0, f), conv_w, conv_b, (0, f), f, length, BF16, tm, 512,
                                  CONV_SUB_ROWS)
        new_state = jnp.concatenate([_tile_tails(t, rows // length, length, tm, width - 1) for t in tails],
                                    axis=-1)
    else:
        hid = _matmul_small(x_bf16, w_up, layer, SMALL_TK)
        act = _conv_decode(hid, jnp.swapaxes(conv_state, 0, 1), conv_w, conv_b, True, BF16, 512)
        new_state = jnp.concatenate([conv_state[:, 1:], hid[:, None]], axis=1)
    xf, xb = _matmul_res_ln([act], w_down, layer, x_f32, ln_g, ln_b, alpha, RES_LN_ROWS)
    return xf, xb, new_state


def kernel(x_prompt, x_sample, cache_k, cache_v, page_table, state_ret, state_ssm, state_ssm_conv,
           state_ffn_conv, ln1_g, ln1_b, ln2_g, ln2_b, w_in_ab, w_out_ab, sb_bias, w_in_ssd, conv_w_ssd,
           conv_b_ssd, dt_bias, a_log, d_skip, norm_w_ssd, w_out_ssd, w_up, conv_w_ffn, conv_b_ffn, w_down):
    bp, seq, d_model = x_prompt.shape
    bs = x_sample.shape[0]
    depth = ln1_g.shape[0]
    alpha = (2 * depth) ** 0.25
    past_len = page_table.shape[1] * cache_k.shape[2]
    sb_heads, sb_dim = cache_k.shape[3], cache_k.shape[4]
    sb_w = sb_heads * sb_dim
    ret_heads, ret_dk, ret_dv = state_ret.shape[2], state_ret.shape[3], state_ret.shape[4]
    ret_qk_w, ret_v_w = ret_heads * ret_dk, ret_heads * ret_dv
    ssm_heads, ssm_hdim, ssm_dstate = state_ssm.shape[2], state_ssm.shape[3], state_ssm.shape[4]
    d_inner = ssm_heads * ssm_hdim
    conv_dim = state_ssm_conv.shape[-1]
    groups = (conv_dim - d_inner) // (2 * ssm_dstate)
    hpg = ssm_heads // groups
    ssm_width = conv_w_ssd.shape[1]
    gs = groups * ssm_dstate
    main_w = d_inner + conv_dim

    w_in_ab_b, w_out_ab_b = w_in_ab.astype(BF16), w_out_ab.astype(BF16)
    w_in_ssd_b, w_out_ssd_b = w_in_ssd.astype(BF16), w_out_ssd.astype(BF16)
    w_dt_b = w_in_ssd_b[:, :, main_w:]
    w_up_b, w_down_b = w_up.astype(BF16), w_down.astype(BF16)

    xp_f = x_prompt.reshape(bp * seq, d_model)
    xs_f = x_sample.reshape(bs, d_model)
    xp_b = xp_f.astype(BF16)
    xs_b = xs_f.astype(BF16)

    outs = dict(k_p=[], v_p=[], k_s=[], v_s=[], ret_p=[], ret_s=[], ssm_p=[], ssm_s=[],
                sconv_p=[], sconv_s=[], fconv_p=[], fconv_s=[])
    for layer in range(depth):
        idx = layer // 2
        if layer % 2 == 0:
            q_off, k_off, v_off, r_off = 0, sb_w, 2 * sb_w, 3 * sb_w
            ret_w = 2 * ret_qk_w + 2 * ret_v_w
            rk_c, rv_c, rg_c = ret_qk_w, 2 * ret_qk_w, 2 * ret_qk_w + ret_v_w
            q = _matmul(xp_b, w_in_ab_b, idx, BF16, 1024, 1024, q_off, sb_w)
            k = _matmul(xp_b, w_in_ab_b, idx, F32, 1024, 1024, k_off, sb_w)
            v = _matmul(xp_b, w_in_ab_b, idx, F32, 1024, 1024, v_off, sb_w)
            ret = _matmul(xp_b, w_in_ab_b, idx, F32, 1024, 1024, r_off, ret_w)
            ya = _sb_prompt(q, k, v, sb_bias[idx], bp, seq, sb_heads)
            yr, s_new = _ret_prompt(ret, bp, seq, ret_heads, ret_dk, ret_dv, 0, rk_c, rv_c, rg_c)
            outs['k_p'].append(k.reshape(bp, seq, sb_heads, sb_dim))
            outs['v_p'].append(v.reshape(bp, seq, sb_heads, sb_dim))
            outs['ret_p'].append(s_new)
            xp_f, xp_b = _matmul_res_ln([ya, yr], w_out_ab_b, idx, xp_f, ln1_g[layer], ln1_b[layer], alpha,
                                        RES_LN_ROWS)
            proj = _matmul_small(xs_b, w_in_ab_b, idx, SMALL_TK)
            q, k, v = (proj[:, o:o + sb_w] for o in (q_off, k_off, v_off))
            ret = proj[:, r_off:]
            ya = _sb_decode(q.reshape(bs, sb_heads, sb_dim), sb_bias[idx], cache_k[idx], cache_v[idx],
                            page_table, page_table.shape[1])
            yr, s_new = _ret_decode(ret[:, :rk_c].reshape(bs, ret_heads, ret_dk),
                                    ret[:, rk_c:rv_c].reshape(bs, ret_heads, ret_dk),
                                    ret[:, rv_c:rg_c].reshape(bs, ret_heads, ret_dv),
                                    ret[:, rg_c:].reshape(bs, ret_heads, ret_dv),
                                    state_ret[idx], past_len, 8)
            outs['k_s'].append(k.reshape(bs, 1, sb_heads, sb_dim))
            outs['v_s'].append(v.reshape(bs, 1, sb_heads, sb_dim))
            outs['ret_s'].append(s_new)
            xs_f, xs_b = _matmul_res_ln([ya.reshape(bs, sb_w).astype(BF16), yr.reshape(bs, ret_v_w).astype(BF16)],
                                        w_out_ab_b, idx, xs_f, ln1_g[layer], ln1_b[layer], alpha, RES_LN_ROWS)
        else:
            tm = min(CONV_ROWS, seq)
            z = _matmul(xp_b, w_in_ssd_b, idx, F32, 1024, 1024, 0, d_inner)
            dt_raw = _matmul(xp_b, w_dt_b, idx, F32, 1024, ssm_heads)
            xbc, tails = _matmul_conv(xp_b, w_in_ssd_b, idx, (d_inner,), conv_w_ssd[idx], conv_b_ssd[idx], (0,),
                                      conv_dim, seq, F32, tm, 512, CONV_SUB_ROWS)
            y, h_new = _ssd_prompt(xbc, z, dt_raw, dt_bias[idx], a_log[idx], d_skip[idx], norm_w_ssd[idx],
                                   bp, seq, groups, hpg, ssm_hdim, ssm_dstate)
            outs['ssm_p'].append(h_new)
            outs['sconv_p'].append(_tile_tails(tails[0], bp, seq, tm, ssm_width - 1))
            xp_f, xp_b = _matmul_res_ln([y], w_out_ssd_b, idx, xp_f, ln1_g[layer], ln1_b[layer], alpha,
                                        RES_LN_ROWS)
            proj = _matmul_small(xs_b, w_in_ssd_b, idx, SMALL_TK, main_w)
            z, xbc_raw = proj[:, :d_inner], proj[:, d_inner:]
            dt_raw = _matmul(xs_b, w_dt_b, idx, F32, 128, ssm_heads)
            xbc = _conv_decode(xbc_raw, jnp.swapaxes(state_ssm_conv[idx], 0, 1), conv_w_ssd[idx],
                               conv_b_ssd[idx], False, F32, 1024)
            dt, dec = _dt_decode(dt_raw, dt_bias[idx], a_log[idx])
            y, h_new = _ssd_decode(xbc[:, :d_inner], xbc[:, d_inner:d_inner + gs], xbc[:, d_inner + gs:],
                                   z, dt, dec, d_skip[idx], norm_w_ssd[idx], state_ssm[idx],
                                   groups, hpg, ssm_hdim, 4)
            outs['ssm_s'].append(h_new)
            outs['sconv_s'].append(jnp.concatenate([state_ssm_conv[idx][:, 1:], xbc_raw[:, None]], axis=1))
            xs_f, xs_b = _matmul_res_ln([y.astype(BF16)], w_out_ssd_b, idx, xs_f, ln1_g[layer], ln1_b[layer],
                                        alpha, RES_LN_ROWS)

        xp_f, xp_b, fc = _ffn(xp_f, xp_b, None, w_up_b, w_down_b, layer, conv_w_ffn[layer], conv_b_ffn[layer],
                              ln2_g[layer], ln2_b[layer], alpha, seq)
        outs['fconv_p'].append(fc)
        xs_f, xs_b, fc = _ffn(xs_f, xs_b, state_ffn_conv[layer], w_up_b, w_down_b, layer, conv_w_ffn[layer],
                              conv_b_ffn[layer], ln2_g[layer], ln2_b[layer], alpha, 1)
        outs['fconv_s'].append(fc)

    st = lambda key: outs[key][0][None] if len(outs[key]) == 1 else jnp.stack(outs[key])
    return (xp_f.reshape(bp, seq, d_model), xs_f.reshape(bs, 1, d_model),
            st('k_p'), st('v_p'), st('k_s'), st('v_s'), st('ret_p'), st('ret_s'),
            st('ssm_p'), st('ssm_s'), st('sconv_p'), st('sconv_s'), st('fconv_p'), st('fconv_s'))
```
